```python
import math
import jax, jax.numpy as jnp
from jax import lax
import numpy as np

D_MODEL = 2048
BATCH = 4
SEQ = 4096
DEPTH = 2

N_A = DEPTH // 2
N_B = DEPTH - N_A

POOL_WINDOWS = (2, 4, 8, 16)
N_POOL_GROUPS = len(POOL_WINDOWS)
POOL_GROUP = D_MODEL // N_POOL_GROUPS

N_DIFF_HEADS = D_MODEL // 256
DIFF_HEAD_DIM = D_MODEL // N_DIFF_HEADS // 2
V_HEAD_DIM = 2 * DIFF_HEAD_DIM
ROT_DIM = DIFF_HEAD_DIM // 4
ROPE_THETA = 500000.0
Q_BLOCK = 128

D_FF = -(-8 * D_MODEL // (3 * 256)) * 256

EPS = 1e-6

kernel_name = "yoco_pool_diffattn_hybrid"


def rmsnorm(x, g):
    xf = x.astype(jnp.float32)
    y = xf * lax.rsqrt(jnp.mean(xf * xf, axis=-1, keepdims=True) + EPS)
    return (y * g.astype(jnp.float32)).astype(x.dtype)


def lambda_init_for(layer_idx):
    return 0.8 - 0.6 * math.exp(-0.3 * layer_idx)


def rope_tables(s):
    inv = ROPE_THETA ** (-jnp.arange(0, ROT_DIM, 2, dtype=jnp.float32) / ROT_DIM)
    ang = jnp.arange(s, dtype=jnp.float32)[:, None] * inv[None, :]
    return jnp.cos(ang), jnp.sin(ang)


def apply_partial_rope(x, cos, sin):
    half = ROT_DIM // 2
    xr = x[..., :ROT_DIM].astype(jnp.float32)
    x1, x2 = xr[..., :half], xr[..., half:]
    c = cos[None, :, None, :]
    s_ = sin[None, :, None, :]
    rot = jnp.concatenate([x1 * c - x2 * s_, x2 * c + x1 * s_], axis=-1)
    return jnp.concatenate([rot.astype(x.dtype), x[..., ROT_DIM:]], axis=-1)


def pool_mixer(h, w_pool, pool_scale):
    b, s, d = h.shape
    hf = h.astype(jnp.float32).reshape(b, s, N_POOL_GROUPS, POOL_GROUP)
    cs = jnp.cumsum(hf, axis=1)
    pos = jnp.arange(s)
    means = []
    for g, w in enumerate(POOL_WINDOWS):
        csg = cs[:, :, g]
        lagged = jnp.pad(csg, ((0, 0), (w, 0), (0, 0)))[:, :s]
        cnt = jnp.minimum(pos + 1, w).astype(jnp.float32)[None, :, None]
        means.append((csg - lagged) / cnt)
    mixed = (jnp.stack(means, axis=2) - hf).astype(h.dtype)
    y = jnp.einsum('bsgc,gcf->bsgf', mixed, w_pool)
    return y.reshape(b, s, d) * pool_scale


def shared_kv(x, kv_norm, w_kv, cos, sin):
    b, s, _ = x.shape
    kv = rmsnorm(x, kv_norm) @ w_kv
    k = kv[..., :D_MODEL].reshape(b, s, 2 * N_DIFF_HEADS, DIFF_HEAD_DIM)
    v = kv[..., D_MODEL:].reshape(b, s, N_DIFF_HEADS, V_HEAD_DIM)
    k = apply_partial_rope(k, cos, sin)
    return k.transpose(0, 2, 1, 3), v.transpose(0, 2, 1, 3)


def diff_attention(h, k, v, w_q, lam_q1, lam_k1, lam_q2, lam_k2, subln_gain, w_o,
                   lambda_init, cos, sin):
    b, s, _ = h.shape
    q = (h @ w_q).reshape(b, s, 2 * N_DIFF_HEADS, DIFF_HEAD_DIM)
    q = apply_partial_rope(q, cos, sin)
    lam = (jnp.exp(jnp.sum(lam_q1.astype(jnp.float32) * lam_k1.astype(jnp.float32)))
           - jnp.exp(jnp.sum(lam_q2.astype(jnp.float32) * lam_k2.astype(jnp.float32)))
           + lambda_init)
    scale = DIFF_HEAD_DIM ** -0.5
    nblk = s // Q_BLOCK
    qb = q.reshape(b, nblk, Q_BLOCK, 2 * N_DIFF_HEADS, DIFF_HEAD_DIM).transpose(1, 0, 3, 2, 4)
    kpos = jnp.arange(s)

    def one_block(args):
        qblk, i = args
        sc = jnp.einsum('bhqd,bhkd->bhqk', qblk, k,
                        preferred_element_type=jnp.float32) * scale
        qpos = i * Q_BLOCK + jnp.arange(Q_BLOCK)
        sc = jnp.where(kpos[None, :] <= qpos[:, None], sc, -jnp.inf)
        p = jax.nn.softmax(sc, axis=-1).reshape(b, N_DIFF_HEADS, 2, Q_BLOCK, s)
        a = p[:, :, 0] - lam * p[:, :, 1]
        return jnp.einsum('bhqk,bhkd->bhqd', a.astype(v.dtype), v)

    o = lax.map(one_block, (qb, jnp.arange(nblk)))
    o = o.transpose(1, 0, 3, 2, 4).reshape(b, s, N_DIFF_HEADS, V_HEAD_DIM)
    o = rmsnorm(o, subln_gain) * (1.0 - lambda_init)
    return o.reshape(b, s, D_MODEL) @ w_o


def swiglu(h, w_gate, w_up, w_down):
    return (jax.nn.silu(h @ w_gate) * (h @ w_up)) @ w_down


def setup_inputs(seed: int = 0) -> dict:
    key = jax.random.key(seed)
    ks = jax.random.split(key, 20)
    f32 = jnp.float32
    D = D_MODEL

    def gain(k, shape):
        return 1.0 + 0.05 * jax.random.normal(k, shape, f32)

    return {
        "x": jax.random.normal(ks[0], (BATCH, SEQ, D), f32),
        "norm_mix_pre": gain(ks[1], (DEPTH, D)),
        "norm_mix_post": gain(ks[2], (DEPTH, D)),
        "norm_ffn_pre": gain(ks[3], (DEPTH, D)),
        "norm_ffn_post": gain(ks[4], (DEPTH, D)),
        "w_pool": jax.random.normal(ks[5], (N_A, N_POOL_GROUPS, POOL_GROUP, POOL_GROUP), f32) * POOL_GROUP ** -0.5,
        "pool_scale": 1.0 + 0.1 * jax.random.normal(ks[6], (N_A, D), f32),
        "kv_norm": gain(ks[7], (D,)),
        "w_kv": jax.random.normal(ks[8], (D, 2 * D), f32) * D ** -0.5,
        "w_q": jax.random.normal(ks[9], (N_B, D, D), f32) * D ** -0.5,
        "lambda_q1": 0.1 * jax.random.normal(ks[10], (N_B, DIFF_HEAD_DIM), f32),
        "lambda_k1": 0.1 * jax.random.normal(ks[11], (N_B, DIFF_HEAD_DIM), f32),
        "lambda_q2": 0.1 * jax.random.normal(ks[12], (N_B, DIFF_HEAD_DIM), f32),
        "lambda_k2": 0.1 * jax.random.normal(ks[13], (N_B, DIFF_HEAD_DIM), f32),
        "subln_gain": gain(ks[14], (N_B, V_HEAD_DIM)),
        "w_o": jax.random.normal(ks[15], (N_B, D, D), f32) * D ** -0.5,
        "w_ffn_gate": jax.random.normal(ks[16], (DEPTH, D, D_FF), f32) * D ** -0.5,
        "w_ffn_up": jax.random.normal(ks[17], (DEPTH, D, D_FF), f32) * D ** -0.5,
        "w_ffn_down": jax.random.normal(ks[18], (DEPTH, D_FF, D), f32) * D_FF ** -0.5,
    }


def reference(x, norm_mix_pre, norm_mix_post, norm_ffn_pre, norm_ffn_post, w_pool, pool_scale,
              kv_norm, w_kv, w_q, lambda_q1, lambda_k1, lambda_q2, lambda_k2, subln_gain, w_o,
              w_ffn_gate, w_ffn_up, w_ffn_down):
    cos, sin = rope_tables(x.shape[1])
    k_sh, v_sh = None, None
    for l in range(DEPTH):
        if l == N_A:
            k_sh, v_sh = shared_kv(x, kv_norm, w_kv, cos, sin)
        h = rmsnorm(x, norm_mix_pre[l])
        if l < N_A:
            m = pool_mixer(h, w_pool[l], pool_scale[l])
        else:
            j = l - N_A
            m = diff_attention(h, k_sh, v_sh, w_q[j], lambda_q1[j], lambda_k1[j], lambda_q2[j],
                               lambda_k2[j], subln_gain[j], w_o[j], lambda_init_for(l), cos, sin)
        x = x + rmsnorm(m, norm_mix_post[l])
        h = rmsnorm(x, norm_ffn_pre[l])
        x = x + rmsnorm(swiglu(h, w_ffn_gate[l], w_ffn_up[l], w_ffn_down[l]), norm_ffn_post[l])
    return x
```

```python
import functools
import math

import jax
import jax.numpy as jnp
from jax import lax
from jax.experimental import pallas as pl
from jax.experimental.pallas import tpu as pltpu

F32 = jnp.float32
BF16 = jnp.bfloat16

EPS = 1e-6
POOL_WINDOWS = (2, 4, 8, 16)
POOL_HALO = 16
SUB_HEAD_DIM = 128
V_HEAD_DIM = 2 * SUB_HEAD_DIM
ROT_DIM = SUB_HEAD_DIM // 4
ROPE_THETA = 500000.0
LOG2E = 1.4426950408889634

V7X_VMEM_LIMIT_BYTES = 56 * 1024 * 1024


def _rms_scale(x):
    return lax.rsqrt(jnp.mean(x * x, axis=-1, keepdims=True) + EPS)


def _pool_kernel(x_ref, gpre_ref, gpost_ref, w_ref, scale_ref, o_ref, halo_ref, *, ts, gc):
    si = pl.program_id(1)

    @pl.when(si == 0)
    def _():
        halo_ref[...] = jnp.zeros_like(halo_ref)

    x = x_ref[0]
    h = (x * _rms_scale(x)) * gpre_ref[...]
    pos = si * ts + lax.broadcasted_iota(jnp.int32, (ts, 1), 0)
    outs = []
    for g, w in enumerate(POOL_WINDOWS):
        hg = h[:, g * gc:(g + 1) * gc]
        s = jnp.concatenate([halo_ref[:, g * gc:(g + 1) * gc], hg], axis=0)
        k = 1
        while k < w:
            s = s + pltpu.roll(s, k, axis=0)
            k *= 2
        cnt = jnp.minimum(pos + 1, w).astype(F32)
        mixed = s[POOL_HALO:, :] * (1.0 / cnt) - hg
        outs.append(jnp.dot(mixed.astype(BF16), w_ref[g], preferred_element_type=F32))
    halo_ref[...] = h[ts - POOL_HALO:, :]
    m = jnp.concatenate(outs, axis=-1) * scale_ref[...]
    o_ref[0] = x + (m * _rms_scale(m)) * gpost_ref[...]


def _pool_mixer(x, g_pre, g_post, w_pool, pool_scale, *, ts=512):
    b, s, d = x.shape
    ng, gc, _ = w_pool.shape
    assert ng == len(POOL_WINDOWS) and ng * gc == d and s % ts == 0
    vec = pl.BlockSpec((1, d), lambda bi, si: (0, 0))
    return pl.pallas_call(
        functools.partial(_pool_kernel, ts=ts, gc=gc),
        out_shape=jax.ShapeDtypeStruct(x.shape, x.dtype),
        grid=(b, s // ts),
        in_specs=[
            pl.BlockSpec((1, ts, d), lambda bi, si: (bi, si, 0)),
            vec, vec,
            pl.BlockSpec((ng, gc, gc), lambda bi, si: (0, 0, 0)),
            vec,
        ],
        out_specs=pl.BlockSpec((1, ts, d), lambda bi, si: (bi, si, 0)),
        scratch_shapes=[pltpu.VMEM((POOL_HALO, d), F32)],
        compiler_params=pltpu.CompilerParams(
            dimension_semantics=("arbitrary", "arbitrary"),
            vmem_limit_bytes=V7X_VMEM_LIMIT_BYTES),
        name="pool_mixer",
    )(x, g_pre.reshape(1, d), g_post.reshape(1, d), w_pool, pool_scale.reshape(1, d))


def _ffn_kernel(x_ref, gpre_ref, gpost_ref, wg_ref, wu_ref, wd_ref, o_ref, h_ref, acc_ref, *, nj):
    j = pl.program_id(1)

    @pl.when(j == 0)
    def _():
        x = x_ref[...]
        h_ref[...] = ((x * _rms_scale(x)) * gpre_ref[...]).astype(BF16)
        acc_ref[...] = jnp.zeros_like(acc_ref)

    h = h_ref[...]
    gate = jnp.dot(h, wg_ref[...], preferred_element_type=F32)
    up = jnp.dot(h, wu_ref[...], preferred_element_type=F32)
    act = (gate * jax.nn.sigmoid(gate)) * up
    acc_ref[...] += jnp.dot(act.astype(BF16), wd_ref[...], preferred_element_type=F32)

    @pl.when(j == nj - 1)
    def _():
        m = acc_ref[...]
        o_ref[...] = x_ref[...] + (m * _rms_scale(m)) * gpost_ref[...]


def _ffn(x2d, g_pre, g_post, w_gate, w_up, w_down, *, tm=512, tf=512):
    t, d = x2d.shape
    dff = w_gate.shape[1]
    assert t % tm == 0 and dff % tf == 0
    nj = dff // tf
    vec = pl.BlockSpec((1, d), lambda i, j: (0, 0))
    return pl.pallas_call(
        functools.partial(_ffn_kernel, nj=nj),
        out_shape=jax.ShapeDtypeStruct(x2d.shape, x2d.dtype),
        grid=(t // tm, nj),
        in_specs=[
            pl.BlockSpec((tm, d), lambda i, j: (i, 0)),
            vec, vec,
            pl.BlockSpec((d, tf), lambda i, j: (0, j)),
            pl.BlockSpec((d, tf), lambda i, j: (0, j)),
            pl.BlockSpec((tf, d), lambda i, j: (j, 0)),
        ],
        out_specs=pl.BlockSpec((tm, d), lambda i, j: (i, 0)),
        scratch_shapes=[pltpu.VMEM((tm, d), BF16), pltpu.VMEM((tm, d), F32)],
        compiler_params=pltpu.CompilerParams(
            dimension_semantics=("arbitrary", "arbitrary"),
            vmem_limit_bytes=V7X_VMEM_LIMIT_BYTES),
        name="ffn",
    )(x2d, g_pre.reshape(1, d), g_post.reshape(1, d), w_gate, w_up, w_down)


def _rope(y, c, sa, sb):
    return y * c + pltpu.roll(y, SUB_HEAD_DIM - ROT_DIM // 2, axis=1) * sa + pltpu.roll(y, ROT_DIM // 2, axis=1) * sb


def _qkv_kernel(x_ref, gk_ref, gq_ref, w_ref, c_ref, sa_ref, sb_ref, k_ref, v_ref, q_ref,
                hk_ref, hq_ref, *, nk, nv, q_scale, tn):
    j = pl.program_id(1)

    @pl.when(j == 0)
    def _():
        x = x_ref[...]
        xn = x * _rms_scale(x)
        hk_ref[...] = (xn * gk_ref[...]).astype(BF16)
        hq_ref[...] = (xn * gq_ref[...]).astype(BF16)

    def roped(y, scale):
        c, sa, sb = c_ref[...], sa_ref[...], sb_ref[...]
        heads = []
        for hd in range(tn // SUB_HEAD_DIM):
            r = _rope(y[:, hd * SUB_HEAD_DIM:(hd + 1) * SUB_HEAD_DIM], c, sa, sb)
            heads.append(r if scale is None else r * scale)
        return jnp.concatenate(heads, axis=-1)

    @pl.when(j < nk)
    def _():
        y = jnp.dot(hk_ref[...], w_ref[...], preferred_element_type=F32)
        k_ref[...] = roped(y, None).astype(BF16)

    @pl.when((j >= nk) & (j < nk + nv))
    def _():
        v_ref[...] = jnp.dot(hk_ref[...], w_ref[...], preferred_element_type=F32).astype(BF16)

    @pl.when(j >= nk + nv)
    def _():
        y = jnp.dot(hq_ref[...], w_ref[...], preferred_element_type=F32)
        q_ref[...] = roped(y, q_scale).astype(BF16)


def _qkv_proj(x2d, g_kv, g_q, w_cat, rope_c, rope_sa, rope_sb, *, d_k, d_v, d_q, q_scale, tm=512, tn=512):
    t, d = x2d.shape
    s = rope_c.shape[0]
    assert t % tm == 0 and s % tm == 0 and d_k % tn == 0 and d_v % tn == 0 and d_q % tn == 0
    nk, nv, nq = d_k // tn, d_v // tn, d_q // tn
    ns = s // tm
    vec = pl.BlockSpec((1, d), lambda i, j: (0, 0))
    tab = pl.BlockSpec((tm, SUB_HEAD_DIM), lambda i, j: (i % ns, 0))
    k_spec = pl.BlockSpec((tm, tn), lambda i, j: (i, jnp.clip(j, 0, nk - 1)))
    v_spec = pl.BlockSpec((tm, tn), lambda i, j: (i, jnp.clip(j - nk, 0, nv - 1)))
    q_spec = pl.BlockSpec((tm, tn), lambda i, j: (i, jnp.clip(j - nk - nv, 0, nq - 1)))
    out_dt = lambda n: jax.ShapeDtypeStruct((t, n), BF16)
    return pl.pallas_call(
        functools.partial(_qkv_kernel, nk=nk, nv=nv, q_scale=q_scale, tn=tn),
        out_shape=(out_dt(d_k), out_dt(d_v), out_dt(d_q)),
        grid=(t // tm, nk + nv + nq),
        in_specs=[
            pl.BlockSpec((tm, d), lambda i, j: (i, 0)),
            vec, vec,
            pl.BlockSpec((d, tn), lambda i, j: (0, j)),
            tab, tab, tab,
        ],
        out_specs=(k_spec, v_spec, q_spec),
        scratch_shapes=[pltpu.VMEM((tm, d), BF16), pltpu.VMEM((tm, d), BF16)],
        compiler_params=pltpu.CompilerParams(
            dimension_semantics=("arbitrary", "arbitrary"),
            vmem_limit_bytes=V7X_VMEM_LIMIT_BYTES),
        name="qkv_proj",
    )(x2d, g_kv.reshape(1, d), g_q.reshape(1, d), w_cat, rope_c, rope_sa, rope_sb)


def _attn_kernel(q_ref, k_ref, v_ref, lq1_ref, lk1_ref, lq2_ref, lk2_ref, g_ref, o_ref, acc_ref,
                 *, tq, lambda_init):
    qi = pl.program_id(2)
    q = q_ref[0]
    qs = [q[:, i * SUB_HEAD_DIM:(i + 1) * SUB_HEAD_DIM] for i in range(2)]
    acc_ref[...] = jnp.zeros_like(acc_ref)

    def block(ki, carry, masked):
        v = v_ref[0, pl.ds(ki * tq, tq), :]
        new = []
        for i in range(2):
            m, l = carry[2 * i], carry[2 * i + 1]
            k = k_ref[0, pl.ds(ki * tq, tq), i * SUB_HEAD_DIM:(i + 1) * SUB_HEAD_DIM]
            s = lax.dot_general(qs[i], k, (((1,), (1,)), ((), ())), preferred_element_type=F32)
            if masked:
                row = lax.broadcasted_iota(jnp.int32, (tq, tq), 0)
                col = lax.broadcasted_iota(jnp.int32, (tq, tq), 1)
                s = jnp.where(col <= row, s, -jnp.inf)
            m_new = jnp.maximum(m, jnp.max(s, axis=-1, keepdims=True))
            alpha = jnp.exp2(m - m_new)
            p = jnp.exp2(s - m_new)
            l = alpha * l + jnp.sum(p, axis=-1, keepdims=True)
            acc_ref[i] = alpha * acc_ref[i] + jnp.dot(p.astype(BF16), v, preferred_element_type=F32)
            new += [m_new, l]
        return tuple(new)

    m0 = jnp.full((tq, 1), -jnp.inf, F32)
    l0 = jnp.zeros((tq, 1), F32)
    carry = lax.fori_loop(0, qi, lambda ki, c: block(ki, c, False), (m0, l0, m0, l0))
    _, l1, _, l2 = block(qi, carry, True)

    lam = (jnp.exp(jnp.sum(lq1_ref[...] * lk1_ref[...], axis=-1, keepdims=True))
           - jnp.exp(jnp.sum(lq2_ref[...] * lk2_ref[...], axis=-1, keepdims=True))
           + lambda_init)
    o = acc_ref[0] * (1.0 / l1) - lam * (acc_ref[1] * (1.0 / l2))
    o = ((o * _rms_scale(o)) * g_ref[...]) * (1.0 - lambda_init)
    o_ref[0] = o.astype(o_ref.dtype)


def _diff_attn(q, k, v, lq1, lk1, lq2, lk2, subln_gain, lambda_init, *, tq=512):
    b, s, d = q.shape
    nh = d // V_HEAD_DIM
    assert s % tq == 0
    lvec = pl.BlockSpec((1, SUB_HEAD_DIM), lambda bi, hi, qi: (0, 0))
    kv_spec = pl.BlockSpec((1, s, V_HEAD_DIM), lambda bi, hi, qi: (bi, 0, hi))
    qo_spec = pl.BlockSpec((1, tq, V_HEAD_DIM), lambda bi, hi, qi: (bi, qi, hi))
    return pl.pallas_call(
        functools.partial(_attn_kernel, tq=tq, lambda_init=lambda_init),
        out_shape=jax.ShapeDtypeStruct((b, s, d), BF16),
        grid=(b, nh, s // tq),
        in_specs=[qo_spec, kv_spec, kv_spec, lvec, lvec, lvec, lvec,
                  pl.BlockSpec((1, V_HEAD_DIM), lambda bi, hi, qi: (0, 0))],
        out_specs=qo_spec,
        scratch_shapes=[pltpu.VMEM((2, tq, V_HEAD_DIM), F32)],
        compiler_params=pltpu.CompilerParams(
            dimension_semantics=("arbitrary", "arbitrary", "arbitrary"),
            vmem_limit_bytes=V7X_VMEM_LIMIT_BYTES),
        name="diff_attn",
    )(q, k, v, lq1.reshape(1, -1), lk1.reshape(1, -1), lq2.reshape(1, -1), lk2.reshape(1, -1),
      subln_gain.reshape(1, -1))


def _oproj_kernel(a_ref, x_ref, w_ref, g_ref, o_ref):
    m = jnp.dot(a_ref[...], w_ref[...], preferred_element_type=F32)
    o_ref[...] = x_ref[...] + (m * _rms_scale(m)) * g_ref[...]


def _out_proj(a2d, x2d, w_o, g_post, *, tm=512):
    t, d = x2d.shape
    assert t % tm == 0
    row = pl.BlockSpec((tm, d), lambda i: (i, 0))
    return pl.pallas_call(
        _oproj_kernel,
        out_shape=jax.ShapeDtypeStruct(x2d.shape, x2d.dtype),
        grid=(t // tm,),
        in_specs=[row, row,
                  pl.BlockSpec((d, d), lambda i: (0, 0)),
                  pl.BlockSpec((1, d), lambda i: (0, 0))],
        out_specs=row,
        compiler_params=pltpu.CompilerParams(
            dimension_semantics=("arbitrary",),
            vmem_limit_bytes=V7X_VMEM_LIMIT_BYTES),
        name="out_proj",
    )(a2d, x2d, w_o, g_post.reshape(1, d))


def _rope_tables(s):
    half = ROT_DIM // 2
    inv = ROPE_THETA ** (-jnp.arange(0, ROT_DIM, 2, dtype=F32) / ROT_DIM)
    ang = jnp.arange(s, dtype=F32)[:, None] * inv[None, :]
    cos, sin = jnp.cos(ang), jnp.sin(ang)
    rest = SUB_HEAD_DIM - ROT_DIM
    c = jnp.concatenate([cos, cos, jnp.ones((s, rest), F32)], axis=-1)
    sa = jnp.concatenate([-sin, jnp.zeros((s, half + rest), F32)], axis=-1)
    sb = jnp.concatenate([jnp.zeros((s, half), F32), sin, jnp.zeros((s, rest), F32)], axis=-1)
    return c, sa, sb


def _lambda_init_for(layer_idx):
    return 0.8 - 0.6 * math.exp(-0.3 * layer_idx)


def kernel(x, norm_mix_pre, norm_mix_post, norm_ffn_pre, norm_ffn_post, w_pool, pool_scale, kv_norm, w_kv, w_q, lambda_q1, lambda_k1, lambda_q2, lambda_k2, subln_gain, w_o, w_ffn_gate, w_ffn_up, w_ffn_down):
    b, s, d = x.shape
    depth = norm_mix_pre.shape[0]
    n_a = w_pool.shape[0]
    rope_c, rope_sa, rope_sb = _rope_tables(s)
    q_scale = SUB_HEAD_DIM ** -0.5 * LOG2E
    k_sh = v_sh = None
    for l in range(depth):
        if l < n_a:
            x = _pool_mixer(x, norm_mix_pre[l], norm_mix_post[l], w_pool[l].astype(BF16), pool_scale[l])
        else:
            jb = l - n_a
            x2d = x.reshape(b * s, d)
            w_cat = jnp.concatenate([w_kv, w_q[jb]], axis=1).astype(BF16)
            k_l, v_l, q = _qkv_proj(x2d, kv_norm, norm_mix_pre[l], w_cat, rope_c, rope_sa, rope_sb,
                                    d_k=d, d_v=d, d_q=d, q_scale=q_scale)
            if k_sh is None:
                k_sh, v_sh = k_l.reshape(b, s, d), v_l.reshape(b, s, d)
            a = _diff_attn(q.reshape(b, s, d), k_sh, v_sh, lambda_q1[jb], lambda_k1[jb], lambda_q2[jb],
                           lambda_k2[jb], subln_gain[jb], _lambda_init_for(l))
            x = _out_proj(a.reshape(b * s, d), x2d, w_o[jb].astype(BF16), norm_mix_post[l]).reshape(b, s, d)
        x = _ffn(x.reshape(b * s, d), norm_ffn_pre[l], norm_ffn_post[l], w_ffn_gate[l].astype(BF16),
                 w_ffn_up[l].astype(BF16), w_ffn_down[l].astype(BF16)).reshape(b, s, d)
    return x
```

```python
import functools
import math

import jax
import jax.numpy as jnp
from jax import lax
from jax.experimental import pallas as pl
from jax.experimental.pallas import tpu as pltpu

F32 = jnp.float32
BF16 = jnp.bfloat16

EPS = 1e-6
POOL_WINDOWS = (2, 4, 8, 16)
POOL_HALO = 16
SUB_HEAD_DIM = 128
V_HEAD_DIM = 2 * SUB_HEAD_DIM
ROT_DIM = SUB_HEAD_DIM // 4
ROPE_THETA = 500000.0
LOG2E = 1.4426950408889634

V7X_VMEM_LIMIT_BYTES = 56 * 1024 * 1024


def _rms_scale(x):
    return lax.rsqrt(jnp.mean(x * x, axis=-1, keepdims=True) + EPS)


def _pool_kernel(x_ref, gpre_ref, gpost_ref, w_ref, scale_ref, o_ref, halo_ref, *, ts, gc):
    si = pl.program_id(1)

    @pl.when(si == 0)
    def _():
        halo_ref[...] = jnp.zeros_like(halo_ref)

    x = x_ref[0]
    h = (x * _rms_scale(x)) * gpre_ref[...]
    pos = si * ts + lax.broadcasted_iota(jnp.int32, (ts, 1), 0)
    outs = []
    for g, w in enumerate(POOL_WINDOWS):
        hg = h[:, g * gc:(g + 1) * gc]
        s = jnp.concatenate([halo_ref[:, g * gc:(g + 1) * gc], hg], axis=0)
        k = 1
        while k < w:
            s = s + pltpu.roll(s, k, axis=0)
            k *= 2
        cnt = jnp.minimum(pos + 1, w).astype(F32)
        mixed = s[POOL_HALO:, :] * (1.0 / cnt) - hg
        outs.append(jnp.dot(mixed.astype(BF16), w_ref[g], preferred_element_type=F32))
    halo_ref[...] = h[ts - POOL_HALO:, :]
    m = jnp.concatenate(outs, axis=-1) * scale_ref[...]
    o_ref[0] = x + (m * _rms_scale(m)) * gpost_ref[...]


def _pool_mixer(x, g_pre, g_post, w_pool, pool_scale, *, ts=512):
    b, s, d = x.shape
    ng, gc, _ = w_pool.shape
    assert ng == len(POOL_WINDOWS) and ng * gc == d and s % ts == 0
    vec = pl.BlockSpec((1, d), lambda bi, si: (0, 0))
    return pl.pallas_call(
        functools.partial(_pool_kernel, ts=ts, gc=gc),
        out_shape=jax.ShapeDtypeStruct(x.shape, x.dtype),
        grid=(b, s // ts),
        in_specs=[
            pl.BlockSpec((1, ts, d), lambda bi, si: (bi, si, 0)),
            vec, vec,
            pl.BlockSpec((ng, gc, gc), lambda bi, si: (0, 0, 0)),
            vec,
        ],
        out_specs=pl.BlockSpec((1, ts, d), lambda bi, si: (bi, si, 0)),
        scratch_shapes=[pltpu.VMEM((POOL_HALO, d), F32)],
        compiler_params=pltpu.CompilerParams(
            dimension_semantics=("arbitrary", "arbitrary"),
            vmem_limit_bytes=V7X_VMEM_LIMIT_BYTES),
        name="pool_mixer",
    )(x, g_pre.reshape(1, d), g_post.reshape(1, d), w_pool, pool_scale.reshape(1, d))


def _ffn_kernel(x_ref, gpre_ref, gpost_ref, wg_ref, wu_ref, wd_ref, o_ref, h_ref, acc_ref, *, nj):
    j = pl.program_id(1)

    @pl.when(j == 0)
    def _():
        x = x_ref[...]
        h_ref[...] = ((x * _rms_scale(x)) * gpre_ref[...]).astype(BF16)
        acc_ref[...] = jnp.zeros_like(acc_ref)

    h = h_ref[...]
    gate = jnp.dot(h, wg_ref[...], preferred_element_type=F32)
    up = jnp.dot(h, wu_ref[...], preferred_element_type=F32)
    act = (gate * jax.nn.sigmoid(gate)) * up
    acc_ref[...] += jnp.dot(act.astype(BF16), wd_ref[...], preferred_element_type=F32)

    @pl.when(j == nj - 1)
    def _():
        m = acc_ref[...]
        o_ref[...] = x_ref[...] + (m * _rms_scale(m)) * gpost_ref[...]


def _ffn(x2d, g_pre, g_post, w_gate, w_up, w_down, *, tm=512, tf=512):
    t, d = x2d.shape
    dff = w_gate.shape[1]
    assert t % tm == 0 and dff % tf == 0
    nj = dff // tf
    vec = pl.BlockSpec((1, d), lambda i, j: (0, 0))
    return pl.pallas_call(
        functools.partial(_ffn_kernel, nj=nj),
        out_shape=jax.ShapeDtypeStruct(x2d.shape, x2d.dtype),
        grid=(t // tm, nj),
        in_specs=[
            pl.BlockSpec((tm, d), lambda i, j: (i, 0)),
            vec, vec,
            pl.BlockSpec((d, tf), lambda i, j: (0, j)),
            pl.BlockSpec((d, tf), lambda i, j: (0, j)),
            pl.BlockSpec((tf, d), lambda i, j: (j, 0)),
        ],
        out_specs=pl.BlockSpec((tm, d), lambda i, j: (i, 0)),
        scratch_shapes=[pltpu.VMEM((tm, d), BF16), pltpu.VMEM((tm, d), F32)],
        compiler_params=pltpu.CompilerParams(
            dimension_semantics=("arbitrary", "arbitrary"),
            vmem_limit_bytes=V7X_VMEM_LIMIT_BYTES),
        name="ffn",
    )(x2d, g_pre.reshape(1, d), g_post.reshape(1, d), w_gate, w_up, w_down)


def _pair_layout():
    half = ROT_DIM // 2
    perm, member = [], []
    for slab in range(2):
        for lane in range(SUB_HEAD_DIM):
            if lane < ROT_DIM:
                sub, dim = lane // half, lane % half + slab * half
            else:
                sub, dim = slab, lane
            perm.append(sub * SUB_HEAD_DIM + dim)
            member.append(sub)
    return perm, member


def _qkv_kernel(x_ref, gk_ref, gq_ref, w_ref, cos_ref, sin_ref, k_ref, v_ref, q_ref,
                hk_ref, hq_ref, *, nk, nv, q_scale, tn):
    j = pl.program_id(1)

    @pl.when(j == 0)
    def _():
        x = x_ref[...]
        xn = x * _rms_scale(x)
        hk_ref[...] = (xn * gk_ref[...]).astype(BF16)
        hq_ref[...] = (xn * gq_ref[...]).astype(BF16)

    def project(h_ref, out_ref, rope, scale):
        for c0 in range(0, tn, 2 * SUB_HEAD_DIM):
            y = jnp.dot(h_ref[...], w_ref[:, c0:c0 + 2 * SUB_HEAD_DIM], preferred_element_type=F32)
            a, b = y[:, :SUB_HEAD_DIM], y[:, SUB_HEAD_DIM:]
            if rope:
                cos, sin = cos_ref[...], sin_ref[...]
                a, b = a * cos - b * sin, b * cos + a * sin
            if scale is not None:
                a, b = a * scale, b * scale
            out_ref[:, c0:c0 + SUB_HEAD_DIM] = a.astype(BF16)
            out_ref[:, c0 + SUB_HEAD_DIM:c0 + 2 * SUB_HEAD_DIM] = b.astype(BF16)

    @pl.when(j < nk)
    def _():
        project(hk_ref, k_ref, True, None)

    @pl.when((j >= nk) & (j < nk + nv))
    def _():
        project(hk_ref, v_ref, False, None)

    @pl.when(j >= nk + nv)
    def _():
        project(hq_ref, q_ref, True, q_scale)


def _qkv_proj(x2d, g_kv, g_q, w_cat, rope_cos, rope_sin, *, d_k, d_v, d_q, q_scale, tm=512, tn=1024):
    t, d = x2d.shape
    s = rope_cos.shape[0]
    assert t % tm == 0 and s % tm == 0 and d_k % tn == 0 and d_v % tn == 0 and d_q % tn == 0
    nk, nv, nq = d_k // tn, d_v // tn, d_q // tn
    ns = s // tm
    vec = pl.BlockSpec((1, d), lambda i, j: (0, 0))
    tab = pl.BlockSpec((tm, SUB_HEAD_DIM), lambda i, j: (i % ns, 0))
    k_spec = pl.BlockSpec((tm, tn), lambda i, j: (i, jnp.clip(j, 0, nk - 1)))
    v_spec = pl.BlockSpec((tm, tn), lambda i, j: (i, jnp.clip(j - nk, 0, nv - 1)))
    q_spec = pl.BlockSpec((tm, tn), lambda i, j: (i, jnp.clip(j - nk - nv, 0, nq - 1)))
    out_dt = lambda n: jax.ShapeDtypeStruct((t, n), BF16)
    return pl.pallas_call(
        functools.partial(_qkv_kernel, nk=nk, nv=nv, q_scale=q_scale, tn=tn),
        out_shape=(out_dt(d_k), out_dt(d_v), out_dt(d_q)),
        grid=(t // tm, nk + nv + nq),
        in_specs=[
            pl.BlockSpec((tm, d), lambda i, j: (i, 0)),
            vec, vec,
            pl.BlockSpec((d, tn), lambda i, j: (0, j)),
            tab, tab,
        ],
        out_specs=(k_spec, v_spec, q_spec),
        scratch_shapes=[pltpu.VMEM((tm, d), BF16), pltpu.VMEM((tm, d), BF16)],
        compiler_params=pltpu.CompilerParams(
            dimension_semantics=("arbitrary", "arbitrary"),
            vmem_limit_bytes=V7X_VMEM_LIMIT_BYTES),
        name="qkv_proj",
    )(x2d, g_kv.reshape(1, d), g_q.reshape(1, d), w_cat, rope_cos, rope_sin)


def _attn_kernel(q_ref, k_ref, v_ref, member_ref, lq1_ref, lk1_ref, lq2_ref, lk2_ref, g_ref, o_ref,
                 acc_ref, s_ref, *, tq, lambda_init):
    qi = pl.program_id(2)
    q = q_ref[0]
    member = member_ref[...]
    qs = [jnp.where(member == i, q, jnp.zeros_like(q)) for i in range(2)]
    acc_ref[...] = jnp.zeros_like(acc_ref)

    def scores(ki, slot, masked):
        bmax = []
        k = k_ref[0, pl.ds(ki * tq, tq), :]
        for i in range(2):
            s = lax.dot_general(k, qs[i], (((1,), (1,)), ((), ())), preferred_element_type=F32)
            if masked:
                kpos = lax.broadcasted_iota(jnp.int32, (tq, tq), 0)
                qpos = lax.broadcasted_iota(jnp.int32, (tq, tq), 1)
                s = jnp.where(kpos <= qpos, s, -jnp.inf)
            s_ref[slot, i] = s
            bmax.append(jnp.max(s, axis=0, keepdims=True))
        return tuple(bmax)

    def accumulate(ki, slot, state):
        carry, bmax = state[:4], state[4:]
        v = v_ref[0, pl.ds(ki * tq, tq), :]
        new = []
        for i in range(2):
            m, l = carry[2 * i], carry[2 * i + 1]
            m_new = jnp.maximum(m, bmax[i])
            alpha = jnp.exp2(m - m_new)
            p = jnp.exp2(s_ref[slot, i] - m_new)
            l = alpha * l + jnp.sum(p, axis=0, keepdims=True)
            pv = lax.dot_general(v, p.astype(BF16), (((0,), (0,)), ((), ())), preferred_element_type=F32)
            acc_ref[i] = alpha * acc_ref[i] + pv
            new += [m_new, l]
        return tuple(new)

    def before(ki):
        return jnp.where(ki == 0, qi, ki - 1)

    def single(state):
        return accumulate(qi, 0, state) + scores(0, 0, False)

    def pair(j, state):
        ki = odd + 2 * j
        state = accumulate(before(ki), 0, state) + scores(ki, 1, False)
        return accumulate(ki, 1, state) + scores(ki + 1, 0, False)

    m0 = jnp.full((1, tq), -jnp.inf, F32)
    l0 = jnp.zeros((1, tq), F32)
    odd = qi % 2
    state = (m0, l0, m0, l0) + scores(qi, 0, True)
    state = lax.cond(odd == 1, single, lambda st: st, state)
    state = lax.fori_loop(0, qi // 2, pair, state)
    _, l1, _, l2 = accumulate(before(qi), 0, state)

    lam = (jnp.exp(jnp.sum(lq1_ref[...] * lk1_ref[...], axis=-1, keepdims=True))
           - jnp.exp(jnp.sum(lq2_ref[...] * lk2_ref[...], axis=-1, keepdims=True))
           + lambda_init)
    ot = acc_ref[0] * (1.0 / l1) - lam * (acc_ref[1] * (1.0 / l2))
    ot = ot * lax.rsqrt(jnp.mean(ot * ot, axis=0, keepdims=True) + EPS)
    o_ref[0] = ((ot.T * g_ref[...]) * (1.0 - lambda_init)).astype(o_ref.dtype)


def _diff_attn(q, k, v, lq1, lk1, lq2, lk2, subln_gain, lambda_init, *, tq=512):
    b, s, d = q.shape
    nh = d // V_HEAD_DIM
    assert s % tq == 0
    lvec = pl.BlockSpec((1, SUB_HEAD_DIM), lambda bi, hi, qi: (0, 0))
    hvec = pl.BlockSpec((1, V_HEAD_DIM), lambda bi, hi, qi: (0, 0))
    member = jnp.asarray(_pair_layout()[1], jnp.int32).reshape(1, V_HEAD_DIM)
    kv_spec = pl.BlockSpec((1, s, V_HEAD_DIM), lambda bi, hi, qi: (bi, 0, hi))
    qo_spec = pl.BlockSpec((1, tq, V_HEAD_DIM), lambda bi, hi, qi: (bi, qi, hi))
    return pl.pallas_call(
        functools.partial(_attn_kernel, tq=tq, lambda_init=lambda_init),
        out_shape=jax.ShapeDtypeStruct((b, s, d), BF16),
        grid=(b, nh, s // tq),
        in_specs=[qo_spec, kv_spec, kv_spec, hvec, lvec, lvec, lvec, lvec, hvec],
        out_specs=qo_spec,
        scratch_shapes=[pltpu.VMEM((2, V_HEAD_DIM, tq), F32), pltpu.VMEM((2, 2, tq, tq), F32)],
        compiler_params=pltpu.CompilerParams(
            dimension_semantics=("arbitrary", "arbitrary", "arbitrary"),
            vmem_limit_bytes=V7X_VMEM_LIMIT_BYTES),
        name="diff_attn",
    )(q, k, v, member, lq1.reshape(1, -1), lk1.reshape(1, -1), lq2.reshape(1, -1), lk2.reshape(1, -1),
      subln_gain.reshape(1, -1))


def _oproj_kernel(a_ref, x_ref, w_ref, g_ref, o_ref):
    m = jnp.dot(a_ref[...], w_ref[...], preferred_element_type=F32)
    o_ref[...] = x_ref[...] + (m * _rms_scale(m)) * g_ref[...]


def _out_proj(a2d, x2d, w_o, g_post, *, tm=512):
    t, d = x2d.shape
    assert t % tm == 0
    row = pl.BlockSpec((tm, d), lambda i: (i, 0))
    return pl.pallas_call(
        _oproj_kernel,
        out_shape=jax.ShapeDtypeStruct(x2d.shape, x2d.dtype),
        grid=(t // tm,),
        in_specs=[row, row,
                  pl.BlockSpec((d, d), lambda i: (0, 0)),
                  pl.BlockSpec((1, d), lambda i: (0, 0))],
        out_specs=row,
        compiler_params=pltpu.CompilerParams(
            dimension_semantics=("arbitrary",),
            vmem_limit_bytes=V7X_VMEM_LIMIT_BYTES),
        name="out_proj",
    )(a2d, x2d, w_o, g_post.reshape(1, d))


def _rope_tables(s):
    inv = ROPE_THETA ** (-jnp.arange(0, ROT_DIM, 2, dtype=F32) / ROT_DIM)
    ang = jnp.arange(s, dtype=F32)[:, None] * inv[None, :]
    cos, sin = jnp.cos(ang), jnp.sin(ang)
    rest = SUB_HEAD_DIM - ROT_DIM
    cos = jnp.concatenate([cos, cos, jnp.ones((s, rest), F32)], axis=-1)
    sin = jnp.concatenate([sin, sin, jnp.zeros((s, rest), F32)], axis=-1)
    return cos, sin


def _to_pair_layout(w):
    d_in, d_out = w.shape
    perm = jnp.asarray(_pair_layout()[0], jnp.int32)
    return w.reshape(d_in, d_out // V_HEAD_DIM, V_HEAD_DIM)[:, :, perm].reshape(d_in, d_out)


def _lambda_init_for(layer_idx):
    return 0.8 - 0.6 * math.exp(-0.3 * layer_idx)


def kernel(x, norm_mix_pre, norm_mix_post, norm_ffn_pre, norm_ffn_post, w_pool, pool_scale, kv_norm, w_kv, w_q, lambda_q1, lambda_k1, lambda_q2, lambda_k2, subln_gain, w_o, w_ffn_gate, w_ffn_up, w_ffn_down):
    b, s, d = x.shape
    depth = norm_mix_pre.shape[0]
    n_a = w_pool.shape[0]
    rope_cos, rope_sin = _rope_tables(s)
    q_scale = SUB_HEAD_DIM ** -0.5 * LOG2E
    k_sh = v_sh = None
    for l in range(depth):
        if l < n_a:
            x = _pool_mixer(x, norm_mix_pre[l], norm_mix_post[l], w_pool[l].astype(BF16), pool_scale[l])
        else:
            jb = l - n_a
            x2d = x.reshape(b * s, d)
            w_cat = jnp.concatenate([_to_pair_layout(w_kv[:, :d]), w_kv[:, d:], _to_pair_layout(w_q[jb])],
                                    axis=1).astype(BF16)
            k_l, v_l, q = _qkv_proj(x2d, kv_norm, norm_mix_pre[l], w_cat, rope_cos, rope_sin,
                                    d_k=d, d_v=d, d_q=d, q_scale=q_scale)
            if k_sh is None:
                k_sh, v_sh = k_l.reshape(b, s, d), v_l.reshape(b, s, d)
            a = _diff_attn(q.reshape(b, s, d), k_sh, v_sh, lambda_q1[jb], lambda_k1[jb], lambda_q2[jb],
                           lambda_k2[jb], subln_gain[jb], _lambda_init_for(l))
            x = _out_proj(a.reshape(b * s, d), x2d, w_o[jb].astype(BF16), norm_mix_post[l]).reshape(b, s, d)
        x = _ffn(x.reshape(b * s, d), norm_ffn_pre[l], norm_ffn_post[l], w_ffn_gate[l].astype(BF16),
                 w_ffn_up[l].astype(BF16), w_ffn_down[l].astype(BF16)).reshape(b, s, d)
    return x
```

```python
import functools
import math

import jax
import jax.numpy as jnp
from jax import lax
from jax.experimental import pallas as pl
from jax.experimental.pallas import tpu as pltpu

F32 = jnp.float32
BF16 = jnp.bfloat16

EPS = 1e-6
POOL_WINDOWS = (2, 4, 8, 16)
POOL_HALO = 16
SUB_HEAD_DIM = 128
V_HEAD_DIM = 2 * SUB_HEAD_DIM
ROT_DIM = SUB_HEAD_DIM // 4
ROPE_THETA = 500000.0
LOG2E = 1.4426950408889634

V7X_VMEM_LIMIT_BYTES = 56 * 1024 * 1024


def _rms_scale(x):
    return lax.rsqrt(jnp.mean(x * x, axis=-1, keepdims=True) + EPS)


def _pool_kernel(x_ref, gpre_ref, gpost_ref, w_ref, scale_ref, o_ref, halo_ref, *, ts, gc):
    si = pl.program_id(1)

    @pl.when(si == 0)
    def _():
        halo_ref[...] = jnp.zeros_like(halo_ref)

    x = x_ref[0]
    h = (x * _rms_scale(x)) * gpre_ref[...]
    pos = si * ts + lax.broadcasted_iota(jnp.int32, (ts, 1), 0)
    outs = []
    for g, w in enumerate(POOL_WINDOWS):
        hg = h[:, g * gc:(g + 1) * gc]
        s = jnp.concatenate([halo_ref[:, g * gc:(g + 1) * gc], hg], axis=0)
        k = 1
        while k < w:
            s = s + pltpu.roll(s, k, axis=0)
            k *= 2
        cnt = jnp.minimum(pos + 1, w).astype(F32)
        mixed = s[POOL_HALO:, :] * (1.0 / cnt) - hg
        outs.append(jnp.dot(mixed.astype(BF16), w_ref[g], preferred_element_type=F32))
    halo_ref[...] = h[ts - POOL_HALO:, :]
    m = jnp.concatenate(outs, axis=-1) * scale_ref[...]
    o_ref[0] = x + (m * _rms_scale(m)) * gpost_ref[...]


def _pool_mixer(x, g_pre, g_post, w_pool, pool_scale, *, ts=512):
    b, s, d = x.shape
    ng, gc, _ = w_pool.shape
    assert ng == len(POOL_WINDOWS) and ng * gc == d and s % ts == 0
    vec = pl.BlockSpec((1, d), lambda bi, si: (0, 0))
    return pl.pallas_call(
        functools.partial(_pool_kernel, ts=ts, gc=gc),
        out_shape=jax.ShapeDtypeStruct(x.shape, x.dtype),
        grid=(b, s // ts),
        in_specs=[
            pl.BlockSpec((1, ts, d), lambda bi, si: (bi, si, 0)),
            vec, vec,
            pl.BlockSpec((ng, gc, gc), lambda bi, si: (0, 0, 0)),
            vec,
        ],
        out_specs=pl.BlockSpec((1, ts, d), lambda bi, si: (bi, si, 0)),
        scratch_shapes=[pltpu.VMEM((POOL_HALO, d), F32)],
        compiler_params=pltpu.CompilerParams(
            dimension_semantics=("arbitrary", "arbitrary"),
            vmem_limit_bytes=V7X_VMEM_LIMIT_BYTES),
        name="pool_mixer",
    )(x, g_pre.reshape(1, d), g_post.reshape(1, d), w_pool, pool_scale.reshape(1, d))


def _ffn_kernel(x_ref, gpre_ref, gpost_ref, wg_ref, wu_ref, wd_ref, o_ref, h_ref, acc_ref, *, nj):
    j = pl.program_id(1)

    @pl.when(j == 0)
    def _():
        x = x_ref[...]
        h_ref[...] = ((x * _rms_scale(x)) * gpre_ref[...]).astype(BF16)
        acc_ref[...] = jnp.zeros_like(acc_ref)

    h = h_ref[...]
    gate = jnp.dot(h, wg_ref[...], preferred_element_type=F32)
    up = jnp.dot(h, wu_ref[...], preferred_element_type=F32)
    act = (gate * jax.nn.sigmoid(gate)) * up
    acc_ref[...] += jnp.dot(act.astype(BF16), wd_ref[...], preferred_element_type=F32)

    @pl.when(j == nj - 1)
    def _():
        m = acc_ref[...]
        o_ref[...] = x_ref[...] + (m * _rms_scale(m)) * gpost_ref[...]


def _ffn(x2d, g_pre, g_post, w_gate, w_up, w_down, *, tm=512, tf=512):
    t, d = x2d.shape
    dff = w_gate.shape[1]
    assert t % tm == 0 and dff % tf == 0
    nj = dff // tf
    vec = pl.BlockSpec((1, d), lambda i, j: (0, 0))
    return pl.pallas_call(
        functools.partial(_ffn_kernel, nj=nj),
        out_shape=jax.ShapeDtypeStruct(x2d.shape, x2d.dtype),
        grid=(t // tm, nj),
        in_specs=[
            pl.BlockSpec((tm, d), lambda i, j: (i, 0)),
            vec, vec,
            pl.BlockSpec((d, tf), lambda i, j: (0, j)),
            pl.BlockSpec((d, tf), lambda i, j: (0, j)),
            pl.BlockSpec((tf, d), lambda i, j: (j, 0)),
        ],
        out_specs=pl.BlockSpec((tm, d), lambda i, j: (i, 0)),
        scratch_shapes=[pltpu.VMEM((tm, d), BF16), pltpu.VMEM((tm, d), F32)],
        compiler_params=pltpu.CompilerParams(
            dimension_semantics=("arbitrary", "arbitrary"),
            vmem_limit_bytes=V7X_VMEM_LIMIT_BYTES),
        name="ffn",
    )(x2d, g_pre.reshape(1, d), g_post.reshape(1, d), w_gate, w_up, w_down)


def _pair_layout():
    half = ROT_DIM // 2
    perm, member = [], []
    for slab in range(2):
        for lane in range(SUB_HEAD_DIM):
            if lane < ROT_DIM:
                sub, dim = lane // half, lane % half + slab * half
            else:
                sub, dim = slab, lane
            perm.append(sub * SUB_HEAD_DIM + dim)
            member.append(sub)
    return perm, member


def _qkv_kernel(x_ref, gk_ref, gq_ref, w_ref, cos_ref, sin_ref, k_ref, v_ref, q_ref,
                hk_ref, hq_ref, *, nk, nv, q_scale, tn):
    j = pl.program_id(1)

    @pl.when(j == 0)
    def _():
        x = x_ref[...]
        xn = x * _rms_scale(x)
        hk_ref[...] = (xn * gk_ref[...]).astype(BF16)
        hq_ref[...] = (xn * gq_ref[...]).astype(BF16)

    def project(h_ref, out_ref, rope, scale):
        for c0 in range(0, tn, 2 * SUB_HEAD_DIM):
            y = jnp.dot(h_ref[...], w_ref[:, c0:c0 + 2 * SUB_HEAD_DIM], preferred_element_type=F32)
            a, b = y[:, :SUB_HEAD_DIM], y[:, SUB_HEAD_DIM:]
            if rope:
                cos, sin = cos_ref[...], sin_ref[...]
                a, b = a * cos - b * sin, b * cos + a * sin
            if scale is not None:
                a, b = a * scale, b * scale
            out_ref[:, c0:c0 + SUB_HEAD_DIM] = a.astype(BF16)
            out_ref[:, c0 + SUB_HEAD_DIM:c0 + 2 * SUB_HEAD_DIM] = b.astype(BF16)

    @pl.when(j < nk)
    def _():
        project(hk_ref, k_ref, True, None)

    @pl.when((j >= nk) & (j < nk + nv))
    def _():
        project(hk_ref, v_ref, False, None)

    @pl.when(j >= nk + nv)
    def _():
        project(hq_ref, q_ref, True, q_scale)


def _qkv_proj(x2d, g_kv, g_q, w_cat, rope_cos, rope_sin, *, d_k, d_v, d_q, q_scale, tm=512, tn=1024):
    t, d = x2d.shape
    s = rope_cos.shape[0]
    assert t % tm == 0 and s % tm == 0 and d_k % tn == 0 and d_v % tn == 0 and d_q % tn == 0
    nk, nv, nq = d_k // tn, d_v // tn, d_q // tn
    ns = s // tm
    vec = pl.BlockSpec((1, d), lambda i, j: (0, 0))
    tab = pl.BlockSpec((tm, SUB_HEAD_DIM), lambda i, j: (i % ns, 0))
    k_spec = pl.BlockSpec((tm, tn), lambda i, j: (i, jnp.clip(j, 0, nk - 1)))
    v_spec = pl.BlockSpec((tm, tn), lambda i, j: (i, jnp.clip(j - nk, 0, nv - 1)))
    q_spec = pl.BlockSpec((tm, tn), lambda i, j: (i, jnp.clip(j - nk - nv, 0, nq - 1)))
    out_dt = lambda n: jax.ShapeDtypeStruct((t, n), BF16)
    return pl.pallas_call(
        functools.partial(_qkv_kernel, nk=nk, nv=nv, q_scale=q_scale, tn=tn),
        out_shape=(out_dt(d_k), out_dt(d_v), out_dt(d_q)),
        grid=(t // tm, nk + nv + nq),
        in_specs=[
            pl.BlockSpec((tm, d), lambda i, j: (i, 0)),
            vec, vec,
            pl.BlockSpec((d, tn), lambda i, j: (0, j)),
            tab, tab,
        ],
        out_specs=(k_spec, v_spec, q_spec),
        scratch_shapes=[pltpu.VMEM((tm, d), BF16), pltpu.VMEM((tm, d), BF16)],
        compiler_params=pltpu.CompilerParams(
            dimension_semantics=("arbitrary", "arbitrary"),
            vmem_limit_bytes=V7X_VMEM_LIMIT_BYTES),
        name="qkv_proj",
    )(x2d, g_kv.reshape(1, d), g_q.reshape(1, d), w_cat, rope_cos, rope_sin)


def _attn_kernel(q_ref, k_ref, v_ref, member_ref, lq1_ref, lk1_ref, lq2_ref, lk2_ref, g_ref, o_ref,
                 acc_ref, s_ref, *, tq, nq, lambda_init):
    member = member_ref[...]
    lam = (jnp.exp(jnp.sum(lq1_ref[...] * lk1_ref[...], axis=-1, keepdims=True))
           - jnp.exp(jnp.sum(lq2_ref[...] * lk2_ref[...], axis=-1, keepdims=True))
           + lambda_init)

    def q_tile(qi):
        base = qi % 2
        q = q_ref[0, qi * tq:(qi + 1) * tq, :]
        qs = [jnp.where(member == i, q, jnp.zeros_like(q)) for i in range(2)]
        acc_ref[base] = jnp.zeros_like(acc_ref[base])

        def scores(ki, slot, masked):
            bmax = []
            k = k_ref[0, pl.ds(ki * tq, tq), :]
            for i in range(2):
                s = lax.dot_general(k, qs[i], (((1,), (1,)), ((), ())), preferred_element_type=F32)
                if masked:
                    kpos = lax.broadcasted_iota(jnp.int32, (tq, tq), 0)
                    qpos = lax.broadcasted_iota(jnp.int32, (tq, tq), 1)
                    s = jnp.where(kpos <= qpos, s, -jnp.inf)
                s_ref[slot, i] = s
                bmax.append(jnp.max(s, axis=0, keepdims=True))
            return tuple(bmax)

        def accumulate(ki, slot, state):
            carry, bmax = state[:4], state[4:]
            v = v_ref[0, pl.ds(ki * tq, tq), :]
            new = []
            for i in range(2):
                m, l = carry[2 * i], carry[2 * i + 1]
                m_new = jnp.maximum(m, bmax[i])
                alpha = jnp.exp2(m - m_new)
                p = jnp.exp2(s_ref[slot, i] - m_new)
                l = alpha * l + jnp.sum(p, axis=0, keepdims=True)
                pv = lax.dot_general(v, p.astype(BF16), (((0,), (0,)), ((), ())),
                                     preferred_element_type=F32)
                acc_ref[base, i] = alpha * acc_ref[base, i] + pv
                new += [m_new, l]
            return tuple(new)

        def before(ki):
            return jnp.where(ki == 0, qi, ki - 1)

        def pair(j, state):
            ki = qi % 2 + 2 * j
            state = accumulate(before(ki), base, state) + scores(ki, 1 - base, False)
            return accumulate(ki, 1 - base, state) + scores(ki + 1, base, False)

        m0 = jnp.full((1, tq), -jnp.inf, F32)
        l0 = jnp.zeros((1, tq), F32)
        state = (m0, l0, m0, l0) + scores(qi, base, True)
        if qi % 2 == 1:
            state = accumulate(qi, base, state) + scores(0, base, False)
        if qi // 2 > 0:
            state = lax.fori_loop(0, qi // 2, pair, state)
        _, l1, _, l2 = accumulate(max(qi - 1, 0), base, state)

        ot = acc_ref[base, 0] * (1.0 / l1) - lam * (acc_ref[base, 1] * (1.0 / l2))
        ot = ot * lax.rsqrt(jnp.mean(ot * ot, axis=0, keepdims=True) + EPS)
        o_ref[0, qi * tq:(qi + 1) * tq, :] = ((ot.T * g_ref[...]) * (1.0 - lambda_init)).astype(o_ref.dtype)

    for qi in range(nq):
        q_tile(qi)


def _diff_attn(q, k, v, lq1, lk1, lq2, lk2, subln_gain, lambda_init, *, tq=512):
    b, s, d = q.shape
    nh = d // V_HEAD_DIM
    assert s % tq == 0
    lvec = pl.BlockSpec((1, SUB_HEAD_DIM), lambda bi, hi: (0, 0))
    hvec = pl.BlockSpec((1, V_HEAD_DIM), lambda bi, hi: (0, 0))
    member = jnp.asarray(_pair_layout()[1], jnp.int32).reshape(1, V_HEAD_DIM)
    seq = pl.BlockSpec((1, s, V_HEAD_DIM), lambda bi, hi: (bi, 0, hi))
    return pl.pallas_call(
        functools.partial(_attn_kernel, tq=tq, nq=s // tq, lambda_init=lambda_init),
        out_shape=jax.ShapeDtypeStruct((b, s, d), BF16),
        grid=(b, nh),
        in_specs=[seq, seq, seq, hvec, lvec, lvec, lvec, lvec, hvec],
        out_specs=seq,
        scratch_shapes=[pltpu.VMEM((2, 2, V_HEAD_DIM, tq), F32), pltpu.VMEM((2, 2, tq, tq), F32)],
        compiler_params=pltpu.CompilerParams(
            dimension_semantics=("arbitrary", "arbitrary"),
            vmem_limit_bytes=V7X_VMEM_LIMIT_BYTES),
        name="diff_attn",
    )(q, k, v, member, lq1.reshape(1, -1), lk1.reshape(1, -1), lq2.reshape(1, -1), lk2.reshape(1, -1),
      subln_gain.reshape(1, -1))


def _oproj_kernel(a_ref, x_ref, w_ref, g_ref, o_ref):
    m = jnp.dot(a_ref[...], w_ref[...], preferred_element_type=F32)
    o_ref[...] = x_ref[...] + (m * _rms_scale(m)) * g_ref[...]


def _out_proj(a2d, x2d, w_o, g_post, *, tm=512):
    t, d = x2d.shape
    assert t % tm == 0
    row = pl.BlockSpec((tm, d), lambda i: (i, 0))
    return pl.pallas_call(
        _oproj_kernel,
        out_shape=jax.ShapeDtypeStruct(x2d.shape, x2d.dtype),
        grid=(t // tm,),
        in_specs=[row, row,
                  pl.BlockSpec((d, d), lambda i: (0, 0)),
                  pl.BlockSpec((1, d), lambda i: (0, 0))],
        out_specs=row,
        compiler_params=pltpu.CompilerParams(
            dimension_semantics=("arbitrary",),
            vmem_limit_bytes=V7X_VMEM_LIMIT_BYTES),
        name="out_proj",
    )(a2d, x2d, w_o, g_post.reshape(1, d))


def _rope_tables(s):
    inv = ROPE_THETA ** (-jnp.arange(0, ROT_DIM, 2, dtype=F32) / ROT_DIM)
    ang = jnp.arange(s, dtype=F32)[:, None] * inv[None, :]
    cos, sin = jnp.cos(ang), jnp.sin(ang)
    rest = SUB_HEAD_DIM - ROT_DIM
    cos = jnp.concatenate([cos, cos, jnp.ones((s, rest), F32)], axis=-1)
    sin = jnp.concatenate([sin, sin, jnp.zeros((s, rest), F32)], axis=-1)
    return cos, sin


def _wprep_kernel(wkv_ref, wq_ref, o_ref, *, nk, nv, tn):
    j = pl.program_id(0)
    half = ROT_DIM // 2

    def pair_layout(w):
        lane = lax.broadcasted_iota(jnp.int32, (1, SUB_HEAD_DIM), 1)
        for c0 in range(0, tn, V_HEAD_DIM):
            a = w[:, c0:c0 + SUB_HEAD_DIM]
            b = w[:, c0 + SUB_HEAD_DIM:c0 + V_HEAD_DIM]
            s0 = jnp.where((lane >= half) & (lane < ROT_DIM), pltpu.roll(b, half, axis=1), a)
            s1 = jnp.where(lane < half, pltpu.roll(a, SUB_HEAD_DIM - half, axis=1), b)
            o_ref[:, c0:c0 + SUB_HEAD_DIM] = s0.astype(BF16)
            o_ref[:, c0 + SUB_HEAD_DIM:c0 + V_HEAD_DIM] = s1.astype(BF16)

    @pl.when(j < nk)
    def _():
        pair_layout(wkv_ref[...])

    @pl.when((j >= nk) & (j < nk + nv))
    def _():
        o_ref[...] = wkv_ref[...].astype(BF16)

    @pl.when(j >= nk + nv)
    def _():
        pair_layout(wq_ref[...])


def _prep_qkv_weights(w_kv, w_q, *, d_k, tn=512):
    d_in, d_kv = w_kv.shape
    d_q = w_q.shape[1]
    assert d_k % tn == 0 and d_kv % tn == 0 and d_q % tn == 0 and tn % V_HEAD_DIM == 0
    nk, nkv, nq = d_k // tn, d_kv // tn, d_q // tn
    return pl.pallas_call(
        functools.partial(_wprep_kernel, nk=nk, nv=nkv - nk, tn=tn),
        out_shape=jax.ShapeDtypeStruct((d_in, d_kv + d_q), BF16),
        grid=(nkv + nq,),
        in_specs=[pl.BlockSpec((d_in, tn), lambda j: (0, jnp.minimum(j, nkv - 1))),
                  pl.BlockSpec((d_in, tn), lambda j: (0, jnp.maximum(j - nkv, 0)))],
        out_specs=pl.BlockSpec((d_in, tn), lambda j: (0, j)),
        compiler_params=pltpu.CompilerParams(
            dimension_semantics=("arbitrary",),
            vmem_limit_bytes=V7X_VMEM_LIMIT_BYTES),
        name="qkv_weight_prep",
    )(w_kv, w_q)


def _lambda_init_for(layer_idx):
    return 0.8 - 0.6 * math.exp(-0.3 * layer_idx)


def kernel(x, norm_mix_pre, norm_mix_post, norm_ffn_pre, norm_ffn_post, w_pool, pool_scale, kv_norm, w_kv, w_q, lambda_q1, lambda_k1, lambda_q2, lambda_k2, subln_gain, w_o, w_ffn_gate, w_ffn_up, w_ffn_down):
    b, s, d = x.shape
    depth = norm_mix_pre.shape[0]
    n_a = w_pool.shape[0]
    rope_cos, rope_sin = _rope_tables(s)
    q_scale = SUB_HEAD_DIM ** -0.5 * LOG2E
    k_sh = v_sh = None
    for l in range(depth):
        if l < n_a:
            x = _pool_mixer(x, norm_mix_pre[l], norm_mix_post[l], w_pool[l].astype(BF16), pool_scale[l])
        else:
            jb = l - n_a
            x2d = x.reshape(b * s, d)
            w_cat = _prep_qkv_weights(w_kv, w_q[jb], d_k=d)
            k_l, v_l, q = _qkv_proj(x2d, kv_norm, norm_mix_pre[l], w_cat, rope_cos, rope_sin,
                                    d_k=d, d_v=d, d_q=d, q_scale=q_scale)
            if k_sh is None:
                k_sh, v_sh = k_l.reshape(b, s, d), v_l.reshape(b, s, d)
            a = _diff_attn(q.reshape(b, s, d), k_sh, v_sh, lambda_q1[jb], lambda_k1[jb], lambda_q2[jb],
                           lambda_k2[jb], subln_gain[jb], _lambda_init_for(l))
            x = _out_proj(a.reshape(b * s, d), x2d, w_o[jb].astype(BF16), norm_mix_post[l]).reshape(b, s, d)
        x = _ffn(x.reshape(b * s, d), norm_ffn_pre[l], norm_ffn_post[l], w_ffn_gate[l].astype(BF16),
                 w_ffn_up[l].astype(BF16), w_ffn_down[l].astype(BF16)).reshape(b, s, d)
    return x
```

```python
import functools
import math

import jax
import jax.numpy as jnp
from jax import lax
from jax.experimental import pallas as pl
from jax.experimental.pallas import tpu as pltpu

F32 = jnp.float32
BF16 = jnp.bfloat16

EPS = 1e-6
POOL_WINDOWS = (2, 4, 8, 16)
POOL_HALO = 16
SUB_HEAD_DIM = 128
V_HEAD_DIM = 2 * SUB_HEAD_DIM
ROT_DIM = SUB_HEAD_DIM // 4
ROPE_THETA = 500000.0
LOG2E = 1.4426950408889634
ATTN_TILE = 512

V7X_VMEM_LIMIT_BYTES = 56 * 1024 * 1024


def _rms_scale(x):
    return lax.rsqrt(jnp.mean(x * x, axis=-1, keepdims=True) + EPS)


def _pool_kernel(x_ref, gpre_ref, gpost_ref, w_ref, scale_ref, o_ref, halo_ref, *, ts, gc):
    si = pl.program_id(1)

    @pl.when(si == 0)
    def _():
        halo_ref[...] = jnp.zeros_like(halo_ref)

    x = x_ref[0]
    h = (x * _rms_scale(x)) * gpre_ref[...]
    pos = si * ts + lax.broadcasted_iota(jnp.int32, (ts, 1), 0)
    outs = []
    for g, w in enumerate(POOL_WINDOWS):
        hg = h[:, g * gc:(g + 1) * gc]
        s = jnp.concatenate([halo_ref[:, g * gc:(g + 1) * gc], hg], axis=0)
        k = 1
        while k < w:
            s = s + pltpu.roll(s, k, axis=0)
            k *= 2
        cnt = jnp.minimum(pos + 1, w).astype(F32)
        mixed = s[POOL_HALO:, :] * (1.0 / cnt) - hg
        outs.append(jnp.dot(mixed.astype(BF16), w_ref[g], preferred_element_type=F32))
    halo_ref[...] = h[ts - POOL_HALO:, :]
    m = jnp.concatenate(outs, axis=-1) * scale_ref[...]
    o_ref[0] = x + (m * _rms_scale(m)) * gpost_ref[...]


def _pool_mixer(x, g_pre, g_post, w_pool, pool_scale, *, ts=512):
    b, s, d = x.shape
    ng, gc, _ = w_pool.shape
    assert ng == len(POOL_WINDOWS) and ng * gc == d and s % ts == 0
    vec = pl.BlockSpec((1, d), lambda bi, si: (0, 0))
    return pl.pallas_call(
        functools.partial(_pool_kernel, ts=ts, gc=gc),
        out_shape=jax.ShapeDtypeStruct(x.shape, x.dtype),
        grid=(b, s // ts),
        in_specs=[
            pl.BlockSpec((1, ts, d), lambda bi, si: (bi, si, 0)),
            vec, vec,
            pl.BlockSpec((ng, gc, gc), lambda bi, si: (0, 0, 0)),
            vec,
        ],
        out_specs=pl.BlockSpec((1, ts, d), lambda bi, si: (bi, si, 0)),
        scratch_shapes=[pltpu.VMEM((POOL_HALO, d), F32)],
        compiler_params=pltpu.CompilerParams(
            dimension_semantics=("arbitrary", "arbitrary"),
            vmem_limit_bytes=V7X_VMEM_LIMIT_BYTES),
        name="pool_mixer",
    )(x, g_pre.reshape(1, d), g_post.reshape(1, d), w_pool, pool_scale.reshape(1, d))


def _ffn_kernel(x_ref, gpre_ref, gpost_ref, wg_ref, wu_ref, wd_ref, o_ref, h_ref, acc_ref, *, nj):
    j = pl.program_id(1)

    @pl.when(j == 0)
    def _():
        x = x_ref[...]
        h_ref[...] = ((x * _rms_scale(x)) * gpre_ref[...]).astype(BF16)
        acc_ref[...] = jnp.zeros_like(acc_ref)

    h = h_ref[...]
    gate = jnp.dot(h, wg_ref[...], preferred_element_type=F32)
    up = jnp.dot(h, wu_ref[...], preferred_element_type=F32)
    act = (gate * jax.nn.sigmoid(gate)) * up
    acc_ref[...] += jnp.dot(act.astype(BF16), wd_ref[...], preferred_element_type=F32)

    @pl.when(j == nj - 1)
    def _():
        m = acc_ref[...]
        o_ref[...] = x_ref[...] + (m * _rms_scale(m)) * gpost_ref[...]


def _ffn(x2d, g_pre, g_post, w_gate, w_up, w_down, layer, *, tm=512, tf=512):
    t, d = x2d.shape
    dff = w_gate.shape[2]
    assert t % tm == 0 and dff % tf == 0
    nj = dff // tf
    vec = pl.BlockSpec((1, d), lambda i, j: (0, 0))
    return pl.pallas_call(
        functools.partial(_ffn_kernel, nj=nj),
        out_shape=jax.ShapeDtypeStruct(x2d.shape, x2d.dtype),
        grid=(t // tm, nj),
        in_specs=[
            pl.BlockSpec((tm, d), lambda i, j: (i, 0)),
            vec, vec,
            pl.BlockSpec((None, d, tf), lambda i, j: (layer, 0, j)),
            pl.BlockSpec((None, d, tf), lambda i, j: (layer, 0, j)),
            pl.BlockSpec((None, tf, d), lambda i, j: (layer, j, 0)),
        ],
        out_specs=pl.BlockSpec((tm, d), lambda i, j: (i, 0)),
        scratch_shapes=[pltpu.VMEM((tm, d), BF16), pltpu.VMEM((tm, d), F32)],
        compiler_params=pltpu.CompilerParams(
            dimension_semantics=("arbitrary", "arbitrary"),
            vmem_limit_bytes=V7X_VMEM_LIMIT_BYTES),
        name="ffn",
    )(x2d, g_pre.reshape(1, d), g_post.reshape(1, d), w_gate, w_up, w_down)


def _pair_layout():
    half = ROT_DIM // 2
    perm, member = [], []
    for slab in range(2):
        for lane in range(SUB_HEAD_DIM):
            if lane < ROT_DIM:
                sub, dim = lane // half, lane % half + slab * half
            else:
                sub, dim = slab, lane
            perm.append(sub * SUB_HEAD_DIM + dim)
            member.append(sub)
    return perm, member


def _qkv_kernel(x_ref, gk_ref, gq_ref, w_ref, cos_ref, sin_ref, k_ref, v_ref, q_ref,
                hk_ref, hq_ref, *, nk, nv, q_scale, tn, tile):
    j = pl.program_id(1)

    @pl.when(j == 0)
    def _():
        x = x_ref[...]
        xn = x * _rms_scale(x)
        hk_ref[...] = (xn * gk_ref[...]).astype(BF16)
        hq_ref[...] = (xn * gq_ref[...]).astype(BF16)

    def project(h_ref, out_ref, rope, scale, transposed):
        for c0 in range(0, tn, 2 * SUB_HEAD_DIM):
            y = jnp.dot(h_ref[...], w_ref[:, c0:c0 + 2 * SUB_HEAD_DIM], preferred_element_type=F32)
            a, b = y[:, :SUB_HEAD_DIM], y[:, SUB_HEAD_DIM:]
            if rope:
                cos, sin = cos_ref[...], sin_ref[...]
                a, b = a * cos - b * sin, b * cos + a * sin
            if scale is not None:
                a, b = a * scale, b * scale
            if transposed:
                for ti in range(out_ref.shape[0]):
                    rows = slice(ti * tile, (ti + 1) * tile)
                    out_ref[ti, c0:c0 + SUB_HEAD_DIM, :] = a[rows].T.astype(BF16)
                    out_ref[ti, c0 + SUB_HEAD_DIM:c0 + 2 * SUB_HEAD_DIM, :] = b[rows].T.astype(BF16)
            else:
                out_ref[:, c0:c0 + SUB_HEAD_DIM] = a.astype(BF16)
                out_ref[:, c0 + SUB_HEAD_DIM:c0 + 2 * SUB_HEAD_DIM] = b.astype(BF16)

    @pl.when(j < nk)
    def _():
        project(hk_ref, k_ref, True, None, False)

    @pl.when((j >= nk) & (j < nk + nv))
    def _():
        project(hk_ref, v_ref, False, None, True)

    @pl.when(j >= nk + nv)
    def _():
        project(hq_ref, q_ref, True, q_scale, True)


def _qkv_proj(x2d, g_kv, g_q, w_cat, rope_cos, rope_sin, *, d_k, d_v, d_q, q_scale, tile, tm=512, tn=1024):
    t, d = x2d.shape
    s = rope_cos.shape[0]
    assert t % tm == 0 and s % tm == 0 and d_k % tn == 0 and d_v % tn == 0 and d_q % tn == 0 and tm % tile == 0
    nk, nv, nq = d_k // tn, d_v // tn, d_q // tn
    ns = s // tm
    vec = pl.BlockSpec((1, d), lambda i, j: (0, 0))
    tab = pl.BlockSpec((tm, SUB_HEAD_DIM), lambda i, j: (i % ns, 0))
    k_spec = pl.BlockSpec((tm, tn), lambda i, j: (i, jnp.clip(j, 0, nk - 1)))
    v_spec = pl.BlockSpec((tm // tile, tn, tile), lambda i, j: (i, jnp.clip(j - nk, 0, nv - 1), 0))
    q_spec = pl.BlockSpec((tm // tile, tn, tile), lambda i, j: (i, jnp.clip(j - nk - nv, 0, nq - 1), 0))
    tiled = lambda n: jax.ShapeDtypeStruct((t // tile, n, tile), BF16)
    return pl.pallas_call(
        functools.partial(_qkv_kernel, nk=nk, nv=nv, q_scale=q_scale, tn=tn, tile=tile),
        out_shape=(jax.ShapeDtypeStruct((t, d_k), BF16), tiled(d_v), tiled(d_q)),
        grid=(t // tm, nk + nv + nq),
        in_specs=[
            pl.BlockSpec((tm, d), lambda i, j: (i, 0)),
            vec, vec,
            pl.BlockSpec((d, tn), lambda i, j: (0, j)),
            tab, tab,
        ],
        out_specs=(k_spec, v_spec, q_spec),
        scratch_shapes=[pltpu.VMEM((tm, d), BF16), pltpu.VMEM((tm, d), BF16)],
        compiler_params=pltpu.CompilerParams(
            dimension_semantics=("arbitrary", "arbitrary"),
            vmem_limit_bytes=V7X_VMEM_LIMIT_BYTES),
        name="qkv_proj",
    )(x2d, g_kv.reshape(1, d), g_q.reshape(1, d), w_cat, rope_cos, rope_sin)


def _attn_kernel(q_ref, k_ref, v_ref, member_ref, lq1_ref, lk1_ref, lq2_ref, lk2_ref, g_ref, o_ref,
                 acc_ref, s_ref, *, tq, nq, lambda_init):
    member = member_ref[...]
    lam = (jnp.exp(jnp.sum(lq1_ref[...] * lk1_ref[...], axis=-1, keepdims=True))
           - jnp.exp(jnp.sum(lq2_ref[...] * lk2_ref[...], axis=-1, keepdims=True))
           + lambda_init)

    def q_tile(qi):
        base = qi % 2
        q = q_ref[0, qi]
        qs = [jnp.where(member == i, q, jnp.zeros_like(q)) for i in range(2)]
        acc_ref[base] = jnp.zeros_like(acc_ref[base])

        def scores(ki, slot, masked):
            bmax = []
            k = k_ref[0, pl.ds(ki * tq, tq), :]
            for i in range(2):
                s = jnp.dot(k, qs[i], preferred_element_type=F32)
                if masked:
                    kpos = lax.broadcasted_iota(jnp.int32, (tq, tq), 0)
                    qpos = lax.broadcasted_iota(jnp.int32, (tq, tq), 1)
                    s = jnp.where(kpos <= qpos, s, -jnp.inf)
                s_ref[slot, i] = s
                bmax.append(jnp.max(s, axis=0, keepdims=True))
            return tuple(bmax)

        def accumulate(ki, slot, state):
            carry, bmax = state[:4], state[4:]
            v = v_ref[0, ki]
            new = []
            for i in range(2):
                m, l = carry[2 * i], carry[2 * i + 1]
                m_new = jnp.maximum(m, bmax[i])
                alpha = jnp.exp2(m - m_new)
                p = jnp.exp2(s_ref[slot, i] - m_new)
                l = alpha * l + jnp.sum(p, axis=0, keepdims=True)
                pv = jnp.dot(v, p.astype(BF16), preferred_element_type=F32)
                acc_ref[base, i] = alpha * acc_ref[base, i] + pv
                new += [m_new, l]
            return tuple(new)

        def before(ki):
            return jnp.where(ki == 0, qi, ki - 1)

        def pair(j, state):
            ki = qi % 2 + 2 * j
            state = accumulate(before(ki), base, state) + scores(ki, 1 - base, False)
            return accumulate(ki, 1 - base, state) + scores(ki + 1, base, False)

        m0 = jnp.full((1, tq), -jnp.inf, F32)
        l0 = jnp.zeros((1, tq), F32)
        state = (m0, l0, m0, l0) + scores(qi, base, True)
        if qi % 2 == 1:
            state = accumulate(qi, base, state) + scores(0, base, False)
        if qi // 2 > 0:
            state = lax.fori_loop(0, qi // 2, pair, state)
        _, l1, _, l2 = accumulate(max(qi - 1, 0), base, state)

        ot = acc_ref[base, 0] * (1.0 / l1) - lam * (acc_ref[base, 1] * (1.0 / l2))
        ot = ot * lax.rsqrt(jnp.mean(ot * ot, axis=0, keepdims=True) + EPS)
        o_ref[0, qi * tq:(qi + 1) * tq, :] = ((ot.T * g_ref[...]) * (1.0 - lambda_init)).astype(o_ref.dtype)

    for qi in range(nq):
        q_tile(qi)


def _diff_attn(q, k, v, lq1, lk1, lq2, lk2, subln_gain, lambda_init, *, tq=512):
    b, s, d = k.shape
    nh = d // V_HEAD_DIM
    nq = s // tq
    assert q.shape == v.shape == (b, nq, d, tq)
    lvec = pl.BlockSpec((1, SUB_HEAD_DIM), lambda bi, hi: (0, 0))
    hvec = pl.BlockSpec((1, V_HEAD_DIM), lambda bi, hi: (0, 0))
    member = jnp.asarray(_pair_layout()[1], jnp.int32).reshape(V_HEAD_DIM, 1)
    seq = pl.BlockSpec((1, s, V_HEAD_DIM), lambda bi, hi: (bi, 0, hi))
    tiles = pl.BlockSpec((1, nq, V_HEAD_DIM, tq), lambda bi, hi: (bi, 0, hi, 0))
    return pl.pallas_call(
        functools.partial(_attn_kernel, tq=tq, nq=nq, lambda_init=lambda_init),
        out_shape=jax.ShapeDtypeStruct((b, s, d), BF16),
        grid=(b, nh),
        in_specs=[tiles, seq, tiles, pl.BlockSpec((V_HEAD_DIM, 1), lambda bi, hi: (0, 0)),
                  lvec, lvec, lvec, lvec, hvec],
        out_specs=seq,
        scratch_shapes=[pltpu.VMEM((2, 2, V_HEAD_DIM, tq), F32), pltpu.VMEM((2, 2, tq, tq), F32)],
        compiler_params=pltpu.CompilerParams(
            dimension_semantics=("arbitrary", "arbitrary"),
            vmem_limit_bytes=V7X_VMEM_LIMIT_BYTES),
        name="diff_attn",
    )(q, k, v, member, lq1.reshape(1, -1), lk1.reshape(1, -1), lq2.reshape(1, -1), lk2.reshape(1, -1),
      subln_gain.reshape(1, -1))


def _oproj_kernel(a_ref, x_ref, w_ref, g_ref, o_ref):
    m = jnp.dot(a_ref[...], w_ref[...], preferred_element_type=F32)
    o_ref[...] = x_ref[...] + (m * _rms_scale(m)) * g_ref[...]


def _out_proj(a2d, x2d, w_o, g_post, *, tm=512):
    t, d = x2d.shape
    assert t % tm == 0
    row = pl.BlockSpec((tm, d), lambda i: (i, 0))
    return pl.pallas_call(
        _oproj_kernel,
        out_shape=jax.ShapeDtypeStruct(x2d.shape, x2d.dtype),
        grid=(t // tm,),
        in_specs=[row, row,
                  pl.BlockSpec((d, d), lambda i: (0, 0)),
                  pl.BlockSpec((1, d), lambda i: (0, 0))],
        out_specs=row,
        compiler_params=pltpu.CompilerParams(
            dimension_semantics=("arbitrary",),
            vmem_limit_bytes=V7X_VMEM_LIMIT_BYTES),
        name="out_proj",
    )(a2d, x2d, w_o, g_post.reshape(1, d))


def _rope_tables(s):
    inv = ROPE_THETA ** (-jnp.arange(0, ROT_DIM, 2, dtype=F32) / ROT_DIM)
    ang = jnp.arange(s, dtype=F32)[:, None] * inv[None, :]
    cos, sin = jnp.cos(ang), jnp.sin(ang)
    rest = SUB_HEAD_DIM - ROT_DIM
    cos = jnp.concatenate([cos, cos, jnp.ones((s, rest), F32)], axis=-1)
    sin = jnp.concatenate([sin, sin, jnp.zeros((s, rest), F32)], axis=-1)
    return cos, sin


def _wprep_kernel(wkv_ref, wq_ref, o_ref, *, nk, nv, tn):
    j = pl.program_id(0)
    half = ROT_DIM // 2

    def pair_layout(w):
        lane = lax.broadcasted_iota(jnp.int32, (1, SUB_HEAD_DIM), 1)
        for c0 in range(0, tn, V_HEAD_DIM):
            a = w[:, c0:c0 + SUB_HEAD_DIM]
            b = w[:, c0 + SUB_HEAD_DIM:c0 + V_HEAD_DIM]
            s0 = jnp.where((lane >= half) & (lane < ROT_DIM), pltpu.roll(b, half, axis=1), a)
            s1 = jnp.where(lane < half, pltpu.roll(a, SUB_HEAD_DIM - half, axis=1), b)
            o_ref[:, c0:c0 + SUB_HEAD_DIM] = s0.astype(BF16)
            o_ref[:, c0 + SUB_HEAD_DIM:c0 + V_HEAD_DIM] = s1.astype(BF16)

    @pl.when(j < nk)
    def _():
        pair_layout(wkv_ref[...])

    @pl.when((j >= nk) & (j < nk + nv))
    def _():
        o_ref[...] = wkv_ref[...].astype(BF16)

    @pl.when(j >= nk + nv)
    def _():
        pair_layout(wq_ref[...])


def _prep_qkv_weights(w_kv, w_q, *, d_k, tn=512):
    d_in, d_kv = w_kv.shape
    d_q = w_q.shape[1]
    assert d_k % tn == 0 and d_kv % tn == 0 and d_q % tn == 0 and tn % V_HEAD_DIM == 0
    nk, nkv, nq = d_k // tn, d_kv // tn, d_q // tn
    return pl.pallas_call(
        functools.partial(_wprep_kernel, nk=nk, nv=nkv - nk, tn=tn),
        out_shape=jax.ShapeDtypeStruct((d_in, d_kv + d_q), BF16),
        grid=(nkv + nq,),
        in_specs=[pl.BlockSpec((d_in, tn), lambda j: (0, jnp.minimum(j, nkv - 1))),
                  pl.BlockSpec((d_in, tn), lambda j: (0, jnp.maximum(j - nkv, 0)))],
        out_specs=pl.BlockSpec((d_in, tn), lambda j: (0, j)),
        compiler_params=pltpu.CompilerParams(
            dimension_semantics=("arbitrary",),
            vmem_limit_bytes=V7X_VMEM_LIMIT_BYTES),
        name="qkv_weight_prep",
    )(w_kv, w_q)


def _lambda_init_for(layer_idx):
    return 0.8 - 0.6 * math.exp(-0.3 * layer_idx)


def kernel(x, norm_mix_pre, norm_mix_post, norm_ffn_pre, norm_ffn_post, w_pool, pool_scale, kv_norm, w_kv, w_q, lambda_q1, lambda_k1, lambda_q2, lambda_k2, subln_gain, w_o, w_ffn_gate, w_ffn_up, w_ffn_down):
    b, s, d = x.shape
    depth = norm_mix_pre.shape[0]
    n_a = w_pool.shape[0]
    rope_cos, rope_sin = _rope_tables(s)
    q_scale = SUB_HEAD_DIM ** -0.5 * LOG2E
    w_gate, w_up, w_down = w_ffn_gate.astype(BF16), w_ffn_up.astype(BF16), w_ffn_down.astype(BF16)
    k_sh = v_sh = None
    for l in range(depth):
        if l < n_a:
            x = _pool_mixer(x, norm_mix_pre[l], norm_mix_post[l], w_pool[l].astype(BF16), pool_scale[l])
        else:
            jb = l - n_a
            x2d = x.reshape(b * s, d)
            w_cat = _prep_qkv_weights(w_kv, w_q[jb], d_k=d)
            k_l, v_l, q = _qkv_proj(x2d, kv_norm, norm_mix_pre[l], w_cat, rope_cos, rope_sin,
                                    d_k=d, d_v=d, d_q=d, q_scale=q_scale, tile=ATTN_TILE)
            tiled = (b, s // ATTN_TILE, d, ATTN_TILE)
            if k_sh is None:
                k_sh, v_sh = k_l.reshape(b, s, d), v_l.reshape(tiled)
            a = _diff_attn(q.reshape(tiled), k_sh, v_sh, lambda_q1[jb], lambda_k1[jb], lambda_q2[jb],
                           lambda_k2[jb], subln_gain[jb], _lambda_init_for(l), tq=ATTN_TILE)
            x = _out_proj(a.reshape(b * s, d), x2d, w_o[jb].astype(BF16), norm_mix_post[l]).reshape(b, s, d)
        x = _ffn(x.reshape(b * s, d), norm_ffn_pre[l], norm_ffn_post[l], w_gate, w_up, w_down, l).reshape(b, s, d)
    return x
```

```python
import functools
import math

import jax
import jax.numpy as jnp
from jax import lax
from jax.experimental import pallas as pl
from jax.experimental.pallas import tpu as pltpu

F32 = jnp.float32
BF16 = jnp.bfloat16

EPS = 1e-6
POOL_WINDOWS = (2, 4, 8, 16)
POOL_HALO = 16
SUB_HEAD_DIM = 128
V_HEAD_DIM = 2 * SUB_HEAD_DIM
ROT_DIM = SUB_HEAD_DIM // 4
ROPE_THETA = 500000.0
LOG2E = 1.4426950408889634
ATTN_TILE = 512

V7X_VMEM_LIMIT_BYTES = 56 * 1024 * 1024


def _rms_scale(x):
    return lax.rsqrt(jnp.mean(x * x, axis=-1, keepdims=True) + EPS)


def _pool_kernel(x_ref, gpre_ref, gpost_ref, w_ref, scale_ref, o_ref, halo_ref, *, ts, gc):
    si = pl.program_id(1)

    @pl.when(si == 0)
    def _():
        halo_ref[...] = jnp.zeros_like(halo_ref)

    x = x_ref[0]
    h = (x * _rms_scale(x)) * gpre_ref[...]
    pos = si * ts + lax.broadcasted_iota(jnp.int32, (ts, 1), 0)
    outs = []
    for g, w in enumerate(POOL_WINDOWS):
        hg = h[:, g * gc:(g + 1) * gc]
        s = jnp.concatenate([halo_ref[:, g * gc:(g + 1) * gc], hg], axis=0)
        k = 1
        while k < w:
            s = s + pltpu.roll(s, k, axis=0)
            k *= 2
        cnt = jnp.minimum(pos + 1, w).astype(F32)
        mixed = s[POOL_HALO:, :] * (1.0 / cnt) - hg
        outs.append(jnp.dot(mixed.astype(BF16), w_ref[g], preferred_element_type=F32))
    halo_ref[...] = h[ts - POOL_HALO:, :]
    m = jnp.concatenate(outs, axis=-1) * scale_ref[...]
    o_ref[0] = x + (m * _rms_scale(m)) * gpost_ref[...]


def _pool_mixer(x, g_pre, g_post, w_pool, pool_scale, *, ts=512):
    b, s, d = x.shape
    ng, gc, _ = w_pool.shape
    assert ng == len(POOL_WINDOWS) and ng * gc == d and s % ts == 0
    vec = pl.BlockSpec((1, d), lambda bi, si: (0, 0))
    return pl.pallas_call(
        functools.partial(_pool_kernel, ts=ts, gc=gc),
        out_shape=jax.ShapeDtypeStruct(x.shape, x.dtype),
        grid=(b, s // ts),
        in_specs=[
            pl.BlockSpec((1, ts, d), lambda bi, si: (bi, si, 0)),
            vec, vec,
            pl.BlockSpec((ng, gc, gc), lambda bi, si: (0, 0, 0)),
            vec,
        ],
        out_specs=pl.BlockSpec((1, ts, d), lambda bi, si: (bi, si, 0)),
        scratch_shapes=[pltpu.VMEM((POOL_HALO, d), F32)],
        compiler_params=pltpu.CompilerParams(
            dimension_semantics=("arbitrary", "arbitrary"),
            vmem_limit_bytes=V7X_VMEM_LIMIT_BYTES),
        name="pool_mixer",
    )(x, g_pre.reshape(1, d), g_post.reshape(1, d), w_pool, pool_scale.reshape(1, d))


def _ffn_kernel(x_ref, gpre_ref, gpost_ref, wg_ref, wu_ref, wd_ref, o_ref, h_ref, acc_ref, *, nj):
    j = pl.program_id(1)
    tm = x_ref.shape[0]
    whole = (slice(0, tm),)
    halves = (slice(0, tm // 2), slice(tm // 2, tm))

    def pre_norm(rows):
        x = x_ref[rows, :]
        h_ref[rows, :] = ((x * _rms_scale(x)) * gpre_ref[...]).astype(BF16)

    def swiglu(rows, first):
        h = h_ref[rows, :]
        gate = jnp.dot(h, wg_ref[...], preferred_element_type=F32)
        up = jnp.dot(h, wu_ref[...], preferred_element_type=F32)
        act = (gate * jax.nn.sigmoid(gate)) * up
        down = jnp.dot(act.astype(BF16), wd_ref[...], preferred_element_type=F32)
        acc_ref[rows, :] = down if first else acc_ref[rows, :] + down

    def finish(rows):
        m = acc_ref[rows, :]
        o_ref[rows, :] = x_ref[rows, :] + (m * _rms_scale(m)) * gpost_ref[...]

    @pl.when(j == 0)
    def _():
        for rows in halves:
            pre_norm(rows)
        for rows in halves:
            swiglu(rows, True)

    @pl.when((j > 0) & (j < nj - 1))
    def _():
        for rows in whole:
            swiglu(rows, False)

    @pl.when(j == nj - 1)
    def _():
        for rows in halves:
            swiglu(rows, False)
        for rows in halves:
            finish(rows)


def _ffn(x2d, g_pre, g_post, w_gate, w_up, w_down, layer, *, tm=512, tf=512):
    t, d = x2d.shape
    dff = w_gate.shape[2]
    assert t % tm == 0 and dff % tf == 0 and dff // tf >= 2
    nj = dff // tf
    vec = pl.BlockSpec((1, d), lambda i, j: (0, 0))
    return pl.pallas_call(
        functools.partial(_ffn_kernel, nj=nj),
        out_shape=jax.ShapeDtypeStruct(x2d.shape, x2d.dtype),
        grid=(t // tm, nj),
        in_specs=[
            pl.BlockSpec((tm, d), lambda i, j: (i, 0)),
            vec, vec,
            pl.BlockSpec((None, d, tf), lambda i, j: (layer, 0, j)),
            pl.BlockSpec((None, d, tf), lambda i, j: (layer, 0, j)),
            pl.BlockSpec((None, tf, d), lambda i, j: (layer, j, 0)),
        ],
        out_specs=pl.BlockSpec((tm, d), lambda i, j: (i, 0)),
        scratch_shapes=[pltpu.VMEM((tm, d), BF16), pltpu.VMEM((tm, d), F32)],
        compiler_params=pltpu.CompilerParams(
            dimension_semantics=("arbitrary", "arbitrary"),
            vmem_limit_bytes=V7X_VMEM_LIMIT_BYTES),
        name="ffn",
    )(x2d, g_pre.reshape(1, d), g_post.reshape(1, d), w_gate, w_up, w_down)


def _pair_layout():
    half = ROT_DIM // 2
    perm, member = [], []
    for slab in range(2):
        for lane in range(SUB_HEAD_DIM):
            if lane < ROT_DIM:
                sub, dim = lane // half, lane % half + slab * half
            else:
                sub, dim = slab, lane
            perm.append(sub * SUB_HEAD_DIM + dim)
            member.append(sub)
    return perm, member


def _qkv_kernel(x_ref, gk_ref, gq_ref, w_ref, cos_ref, sin_ref, k_ref, v_ref, q_ref,
                hk_ref, hq_ref, *, nk, nv, q_scale, tn, tile):
    j = pl.program_id(1)
    groups = [slice(r, r + tile) for r in range(0, x_ref.shape[0], tile)]

    def pre_norm(rows):
        x = x_ref[rows, :]
        xn = x * _rms_scale(x)
        hk_ref[rows, :] = (xn * gk_ref[...]).astype(BF16)
        hq_ref[rows, :] = (xn * gq_ref[...]).astype(BF16)

    def project(h_ref, out_ref, rope, scale, transposed):
        for ti, rows in enumerate(groups):
            for c0 in range(0, tn, 2 * SUB_HEAD_DIM):
                y = jnp.dot(h_ref[rows, :], w_ref[:, c0:c0 + 2 * SUB_HEAD_DIM], preferred_element_type=F32)
                a, b = y[:, :SUB_HEAD_DIM], y[:, SUB_HEAD_DIM:]
                if rope:
                    cos, sin = cos_ref[rows, :], sin_ref[rows, :]
                    a, b = a * cos - b * sin, b * cos + a * sin
                if scale is not None:
                    a, b = a * scale, b * scale
                if transposed:
                    out_ref[ti, c0:c0 + SUB_HEAD_DIM, :] = a.T.astype(BF16)
                    out_ref[ti, c0 + SUB_HEAD_DIM:c0 + 2 * SUB_HEAD_DIM, :] = b.T.astype(BF16)
                else:
                    out_ref[rows, c0:c0 + SUB_HEAD_DIM] = a.astype(BF16)
                    out_ref[rows, c0 + SUB_HEAD_DIM:c0 + 2 * SUB_HEAD_DIM] = b.astype(BF16)

    @pl.when(j == 0)
    def _():
        for rows in groups:
            pre_norm(rows)
        project(hk_ref, k_ref, True, None, False)

    @pl.when((j > 0) & (j < nk))
    def _():
        project(hk_ref, k_ref, True, None, False)

    @pl.when((j >= nk) & (j < nk + nv))
    def _():
        project(hk_ref, v_ref, False, None, True)

    @pl.when(j >= nk + nv)
    def _():
        project(hq_ref, q_ref, True, q_scale, True)


def _qkv_proj(x2d, g_kv, g_q, w_cat, rope_cos, rope_sin, *, d_k, d_v, d_q, q_scale, tile, tm=1024, tn=512):
    t, d = x2d.shape
    s = rope_cos.shape[0]
    assert t % tm == 0 and s % tm == 0 and d_k % tn == 0 and d_v % tn == 0 and d_q % tn == 0 and tm % tile == 0
    nk, nv, nq = d_k // tn, d_v // tn, d_q // tn
    ns = s // tm
    vec = pl.BlockSpec((1, d), lambda i, j: (0, 0))
    tab = pl.BlockSpec((tm, SUB_HEAD_DIM), lambda i, j: (i % ns, 0))
    k_spec = pl.BlockSpec((tm, tn), lambda i, j: (i, jnp.clip(j, 0, nk - 1)))
    v_spec = pl.BlockSpec((tm // tile, tn, tile), lambda i, j: (i, jnp.clip(j - nk, 0, nv - 1), 0))
    q_spec = pl.BlockSpec((tm // tile, tn, tile), lambda i, j: (i, jnp.clip(j - nk - nv, 0, nq - 1), 0))
    tiled = lambda n: jax.ShapeDtypeStruct((t // tile, n, tile), BF16)
    return pl.pallas_call(
        functools.partial(_qkv_kernel, nk=nk, nv=nv, q_scale=q_scale, tn=tn, tile=tile),
        out_shape=(jax.ShapeDtypeStruct((t, d_k), BF16), tiled(d_v), tiled(d_q)),
        grid=(t // tm, nk + nv + nq),
        in_specs=[
            pl.BlockSpec((tm, d), lambda i, j: (i, 0)),
            vec, vec,
            pl.BlockSpec((d, tn), lambda i, j: (0, j)),
            tab, tab,
        ],
        out_specs=(k_spec, v_spec, q_spec),
        scratch_shapes=[pltpu.VMEM((tm, d), BF16), pltpu.VMEM((tm, d), BF16)],
        compiler_params=pltpu.CompilerParams(
            dimension_semantics=("arbitrary", "arbitrary"),
            vmem_limit_bytes=V7X_VMEM_LIMIT_BYTES),
        name="qkv_proj",
    )(x2d, g_kv.reshape(1, d), g_q.reshape(1, d), w_cat, rope_cos, rope_sin)


def _attn_kernel(q_ref, k_ref, v_ref, member_ref, lq1_ref, lk1_ref, lq2_ref, lk2_ref, g_ref, o_ref,
                 acc_ref, s_ref, *, tq, nq, lambda_init):
    member = member_ref[...]
    lam = (jnp.exp(jnp.sum(lq1_ref[...] * lk1_ref[...], axis=-1, keepdims=True))
           - jnp.exp(jnp.sum(lq2_ref[...] * lk2_ref[...], axis=-1, keepdims=True))
           + lambda_init)

    def q_tile(qi):
        base = qi % 2
        q = q_ref[0, qi]
        qs = [jnp.where(member == i, q, jnp.zeros_like(q)) for i in range(2)]
        acc_ref[base] = jnp.zeros_like(acc_ref[base])

        def scores(ki, slot, masked):
            bmax = []
            k = k_ref[0, pl.ds(ki * tq, tq), :]
            for i in range(2):
                s = jnp.dot(k, qs[i], preferred_element_type=F32)
                if masked:
                    kpos = lax.broadcasted_iota(jnp.int32, (tq, tq), 0)
                    qpos = lax.broadcasted_iota(jnp.int32, (tq, tq), 1)
                    s = jnp.where(kpos <= qpos, s, -jnp.inf)
                s_ref[slot, i] = s
                bmax.append(jnp.max(s, axis=0, keepdims=True))
            return tuple(bmax)

        def accumulate(ki, slot, state):
            carry, bmax = state[:4], state[4:]
            v = v_ref[0, ki]
            new = []
            for i in range(2):
                m, l = carry[2 * i], carry[2 * i + 1]
                m_new = jnp.maximum(m, bmax[i])
                alpha = jnp.exp2(m - m_new)
                p = jnp.exp2(s_ref[slot, i] - m_new)
                l = alpha * l + jnp.sum(p, axis=0, keepdims=True)
                pv = jnp.dot(v, p.astype(BF16), preferred_element_type=F32)
                acc_ref[base, i] = alpha * acc_ref[base, i] + pv
                new += [m_new, l]
            return tuple(new)

        def before(ki):
            return jnp.where(ki == 0, qi, ki - 1)

        def pair(j, state):
            ki = qi % 2 + 2 * j
            state = accumulate(before(ki), base, state) + scores(ki, 1 - base, False)
            return accumulate(ki, 1 - base, state) + scores(ki + 1, base, False)

        m0 = jnp.full((1, tq), -jnp.inf, F32)
        l0 = jnp.zeros((1, tq), F32)
        state = (m0, l0, m0, l0) + scores(qi, base, True)
        if qi % 2 == 1:
            state = accumulate(qi, base, state) + scores(0, base, False)
        if qi // 2 > 0:
            state = lax.fori_loop(0, qi // 2, pair, state)
        _, l1, _, l2 = accumulate(max(qi - 1, 0), base, state)

        ot = acc_ref[base, 0] * (1.0 / l1) - lam * (acc_ref[base, 1] * (1.0 / l2))
        ot = ot * lax.rsqrt(jnp.mean(ot * ot, axis=0, keepdims=True) + EPS)
        o_ref[0, qi * tq:(qi + 1) * tq, :] = ((ot.T * g_ref[...]) * (1.0 - lambda_init)).astype(o_ref.dtype)

    for qi in range(nq):
        q_tile(qi)


def _diff_attn(q, k, v, lq1, lk1, lq2, lk2, subln_gain, lambda_init, *, tq=512):
    b, s, d = k.shape
    nh = d // V_HEAD_DIM
    nq = s // tq
    assert q.shape == v.shape == (b, nq, d, tq)
    lvec = pl.BlockSpec((1, SUB_HEAD_DIM), lambda bi, hi: (0, 0))
    hvec = pl.BlockSpec((1, V_HEAD_DIM), lambda bi, hi: (0, 0))
    member = jnp.asarray(_pair_layout()[1], jnp.int32).reshape(V_HEAD_DIM, 1)
    seq = pl.BlockSpec((1, s, V_HEAD_DIM), lambda bi, hi: (bi, 0, hi))
    tiles = pl.BlockSpec((1, nq, V_HEAD_DIM, tq), lambda bi, hi: (bi, 0, hi, 0))
    return pl.pallas_call(
        functools.partial(_attn_kernel, tq=tq, nq=nq, lambda_init=lambda_init),
        out_shape=jax.ShapeDtypeStruct((b, s, d), BF16),
        grid=(b, nh),
        in_specs=[tiles, seq, tiles, pl.BlockSpec((V_HEAD_DIM, 1), lambda bi, hi: (0, 0)),
                  lvec, lvec, lvec, lvec, hvec],
        out_specs=seq,
        scratch_shapes=[pltpu.VMEM((2, 2, V_HEAD_DIM, tq), F32), pltpu.VMEM((2, 2, tq, tq), F32)],
        compiler_params=pltpu.CompilerParams(
            dimension_semantics=("arbitrary", "arbitrary"),
            vmem_limit_bytes=V7X_VMEM_LIMIT_BYTES),
        name="diff_attn",
    )(q, k, v, member, lq1.reshape(1, -1), lk1.reshape(1, -1), lq2.reshape(1, -1), lk2.reshape(1, -1),
      subln_gain.reshape(1, -1))


def _oproj_kernel(a_ref, x_ref, w_ref, g_ref, o_ref):
    tm = x_ref.shape[0]
    for rows in (slice(0, tm // 2), slice(tm // 2, tm)):
        m = jnp.dot(a_ref[rows, :], w_ref[...], preferred_element_type=F32)
        o_ref[rows, :] = x_ref[rows, :] + (m * _rms_scale(m)) * g_ref[...]


def _out_proj(a2d, x2d, w_o, g_post, *, tm=512):
    t, d = x2d.shape
    assert t % tm == 0
    row = pl.BlockSpec((tm, d), lambda i: (i, 0))
    return pl.pallas_call(
        _oproj_kernel,
        out_shape=jax.ShapeDtypeStruct(x2d.shape, x2d.dtype),
        grid=(t // tm,),
        in_specs=[row, row,
                  pl.BlockSpec((d, d), lambda i: (0, 0)),
                  pl.BlockSpec((1, d), lambda i: (0, 0))],
        out_specs=row,
        compiler_params=pltpu.CompilerParams(
            dimension_semantics=("arbitrary",),
            vmem_limit_bytes=V7X_VMEM_LIMIT_BYTES),
        name="out_proj",
    )(a2d, x2d, w_o, g_post.reshape(1, d))


def _rope_tables(s):
    inv = ROPE_THETA ** (-jnp.arange(0, ROT_DIM, 2, dtype=F32) / ROT_DIM)
    ang = jnp.arange(s, dtype=F32)[:, None] * inv[None, :]
    cos, sin = jnp.cos(ang), jnp.sin(ang)
    rest = SUB_HEAD_DIM - ROT_DIM
    cos = jnp.concatenate([cos, cos, jnp.ones((s, rest), F32)], axis=-1)
    sin = jnp.concatenate([sin, sin, jnp.zeros((s, rest), F32)], axis=-1)
    return cos, sin


def _wprep_kernel(wkv_ref, wq_ref, o_ref, *, nk, nv, tn):
    j = pl.program_id(0)
    half = ROT_DIM // 2

    def pair_layout(w):
        lane = lax.broadcasted_iota(jnp.int32, (1, SUB_HEAD_DIM), 1)
        for c0 in range(0, tn, V_HEAD_DIM):
            a = w[:, c0:c0 + SUB_HEAD_DIM]
            b = w[:, c0 + SUB_HEAD_DIM:c0 + V_HEAD_DIM]
            s0 = jnp.where((lane >= half) & (lane < ROT_DIM), pltpu.roll(b, half, axis=1), a)
            s1 = jnp.where(lane < half, pltpu.roll(a, SUB_HEAD_DIM - half, axis=1), b)
            o_ref[:, c0:c0 + SUB_HEAD_DIM] = s0.astype(BF16)
            o_ref[:, c0 + SUB_HEAD_DIM:c0 + V_HEAD_DIM] = s1.astype(BF16)

    @pl.when(j < nk)
    def _():
        pair_layout(wkv_ref[...])

    @pl.when((j >= nk) & (j < nk + nv))
    def _():
        o_ref[...] = wkv_ref[...].astype(BF16)

    @pl.when(j >= nk + nv)
    def _():
        pair_layout(wq_ref[...])


def _prep_qkv_weights(w_kv, w_q, *, d_k, tn=512):
    d_in, d_kv = w_kv.shape
    d_q = w_q.shape[1]
    assert d_k % tn == 0 and d_kv % tn == 0 and d_q % tn == 0 and tn % V_HEAD_DIM == 0
    nk, nkv, nq = d_k // tn, d_kv // tn, d_q // tn
    return pl.pallas_call(
        functools.partial(_wprep_kernel, nk=nk, nv=nkv - nk, tn=tn),
        out_shape=jax.ShapeDtypeStruct((d_in, d_kv + d_q), BF16),
        grid=(nkv + nq,),
        in_specs=[pl.BlockSpec((d_in, tn), lambda j: (0, jnp.minimum(j, nkv - 1))),
                  pl.BlockSpec((d_in, tn), lambda j: (0, jnp.maximum(j - nkv, 0)))],
        out_specs=pl.BlockSpec((d_in, tn), lambda j: (0, j)),
        compiler_params=pltpu.CompilerParams(
            dimension_semantics=("arbitrary",),
            vmem_limit_bytes=V7X_VMEM_LIMIT_BYTES),
        name="qkv_weight_prep",
    )(w_kv, w_q)


def _lambda_init_for(layer_idx):
    return 0.8 - 0.6 * math.exp(-0.3 * layer_idx)


def kernel(x, norm_mix_pre, norm_mix_post, norm_ffn_pre, norm_ffn_post, w_pool, pool_scale, kv_norm, w_kv, w_q, lambda_q1, lambda_k1, lambda_q2, lambda_k2, subln_gain, w_o, w_ffn_gate, w_ffn_up, w_ffn_down):
    b, s, d = x.shape
    depth = norm_mix_pre.shape[0]
    n_a = w_pool.shape[0]
    rope_cos, rope_sin = _rope_tables(s)
    q_scale = SUB_HEAD_DIM ** -0.5 * LOG2E
    w_gate, w_up, w_down = w_ffn_gate.astype(BF16), w_ffn_up.astype(BF16), w_ffn_down.astype(BF16)
    k_sh = v_sh = None
    for l in range(depth):
        if l < n_a:
            x = _pool_mixer(x, norm_mix_pre[l], norm_mix_post[l], w_pool[l].astype(BF16), pool_scale[l])
        else:
            jb = l - n_a
            x2d = x.reshape(b * s, d)
            w_cat = _prep_qkv_weights(w_kv, w_q[jb], d_k=d)
            k_l, v_l, q = _qkv_proj(x2d, kv_norm, norm_mix_pre[l], w_cat, rope_cos, rope_sin,
                                    d_k=d, d_v=d, d_q=d, q_scale=q_scale, tile=ATTN_TILE)
            tiled = (b, s // ATTN_TILE, d, ATTN_TILE)
            if k_sh is None:
                k_sh, v_sh = k_l.reshape(b, s, d), v_l.reshape(tiled)
            a = _diff_attn(q.reshape(tiled), k_sh, v_sh, lambda_q1[jb], lambda_k1[jb], lambda_q2[jb],
                           lambda_k2[jb], subln_gain[jb], _lambda_init_for(l), tq=ATTN_TILE)
            x = _out_proj(a.reshape(b * s, d), x2d, w_o[jb].astype(BF16), norm_mix_post[l]).reshape(b, s, d)
        x = _ffn(x.reshape(b * s, d), norm_ffn_pre[l], norm_ffn_post[l], w_gate, w_up, w_down, l).reshape(b, s, d)
    return x
```

```python
import functools
import math

import jax
import jax.numpy as jnp
from jax import lax
from jax.experimental import pallas as pl
from jax.experimental.pallas import tpu as pltpu

F32 = jnp.float32
BF16 = jnp.bfloat16

EPS = 1e-6
POOL_WINDOWS = (2, 4, 8, 16)
POOL_HALO = 16
SUB_HEAD_DIM = 128
V_HEAD_DIM = 2 * SUB_HEAD_DIM
ROT_DIM = SUB_HEAD_DIM // 4
ROPE_THETA = 500000.0
LOG2E = 1.4426950408889634
ATTN_TILE = 512

V7X_VMEM_LIMIT_BYTES = 56 * 1024 * 1024


def _rms_scale(x):
    return lax.rsqrt(jnp.mean(x * x, axis=-1, keepdims=True) + EPS)


def _pool_kernel(x_ref, gpre_ref, gpost_ref, w_ref, scale_ref, o_ref, halo_ref, *, ts, gc):
    si = pl.program_id(1)

    @pl.when(si == 0)
    def _():
        halo_ref[...] = jnp.zeros_like(halo_ref)

    x = x_ref[0]
    h = (x * _rms_scale(x)) * gpre_ref[...]
    pos = si * ts + lax.broadcasted_iota(jnp.int32, (ts, 1), 0)
    outs = []
    for g, w in enumerate(POOL_WINDOWS):
        hg = h[:, g * gc:(g + 1) * gc]
        s = jnp.concatenate([halo_ref[:, g * gc:(g + 1) * gc], hg], axis=0)
        k = 1
        while k < w:
            s = s + pltpu.roll(s, k, axis=0)
            k *= 2
        cnt = jnp.minimum(pos + 1, w).astype(F32)
        mixed = s[POOL_HALO:, :] * (1.0 / cnt) - hg
        outs.append(jnp.dot(mixed.astype(BF16), w_ref[g], preferred_element_type=F32))
    halo_ref[...] = h[ts - POOL_HALO:, :]
    m = jnp.concatenate(outs, axis=-1) * scale_ref[...]
    o_ref[0] = x + (m * _rms_scale(m)) * gpost_ref[...]


def _pool_mixer(x, g_pre, g_post, w_pool, pool_scale, *, ts=512):
    b, s, d = x.shape
    ng, gc, _ = w_pool.shape
    assert ng == len(POOL_WINDOWS) and ng * gc == d and s % ts == 0
    vec = pl.BlockSpec((1, d), lambda bi, si: (0, 0))
    return pl.pallas_call(
        functools.partial(_pool_kernel, ts=ts, gc=gc),
        out_shape=jax.ShapeDtypeStruct(x.shape, x.dtype),
        grid=(b, s // ts),
        in_specs=[
            pl.BlockSpec((1, ts, d), lambda bi, si: (bi, si, 0)),
            vec, vec,
            pl.BlockSpec((ng, gc, gc), lambda bi, si: (0, 0, 0)),
            vec,
        ],
        out_specs=pl.BlockSpec((1, ts, d), lambda bi, si: (bi, si, 0)),
        scratch_shapes=[pltpu.VMEM((POOL_HALO, d), F32)],
        compiler_params=pltpu.CompilerParams(
            dimension_semantics=("arbitrary", "arbitrary"),
            vmem_limit_bytes=V7X_VMEM_LIMIT_BYTES),
        name="pool_mixer",
    )(x, g_pre.reshape(1, d), g_post.reshape(1, d), w_pool, pool_scale.reshape(1, d))


def _ffn_kernel(x_ref, gpre_ref, gpost_ref, wg_ref, wu_ref, wd_ref, o_ref, h_ref, *, nj):
    j = pl.program_id(1)
    tm = x_ref.shape[0]
    whole = (slice(0, tm),)
    halves = (slice(0, tm // 2), slice(tm // 2, tm))

    def pre_norm(rows):
        x = x_ref[rows, :]
        h_ref[rows, :] = ((x * _rms_scale(x)) * gpre_ref[...]).astype(BF16)

    def swiglu(rows, first):
        h = h_ref[rows, :]
        gate = jnp.dot(h, wg_ref[...], preferred_element_type=F32)
        up = jnp.dot(h, wu_ref[...], preferred_element_type=F32)
        act = (gate * jax.nn.sigmoid(gate)) * up
        down = jnp.dot(act.astype(BF16), wd_ref[...], preferred_element_type=F32)
        o_ref[rows, :] = down if first else o_ref[rows, :] + down

    def finish(rows):
        m = o_ref[rows, :]
        o_ref[rows, :] = x_ref[rows, :] + (m * _rms_scale(m)) * gpost_ref[...]

    @pl.when(j == 0)
    def _():
        for rows in halves:
            pre_norm(rows)
        for rows in halves:
            swiglu(rows, True)

    @pl.when((j > 0) & (j < nj - 1))
    def _():
        for rows in whole:
            swiglu(rows, False)

    @pl.when(j == nj - 1)
    def _():
        for rows in halves:
            swiglu(rows, False)
        for rows in halves:
            finish(rows)


def _ffn(x2d, g_pre, g_post, w_gate, w_up, w_down, layer, *, tm=1024, tf=512):
    t, d = x2d.shape
    dff = w_gate.shape[2]
    assert t % tm == 0 and dff % tf == 0 and dff // tf >= 2
    nj = dff // tf
    vec = pl.BlockSpec((1, d), lambda i, j: (0, 0))
    return pl.pallas_call(
        functools.partial(_ffn_kernel, nj=nj),
        out_shape=jax.ShapeDtypeStruct(x2d.shape, x2d.dtype),
        grid=(t // tm, nj),
        in_specs=[
            pl.BlockSpec((tm, d), lambda i, j: (i, 0)),
            vec, vec,
            pl.BlockSpec((None, d, tf), lambda i, j: (layer, 0, j)),
            pl.BlockSpec((None, d, tf), lambda i, j: (layer, 0, j)),
            pl.BlockSpec((None, tf, d), lambda i, j: (layer, j, 0)),
        ],
        out_specs=pl.BlockSpec((tm, d), lambda i, j: (i, 0)),
        scratch_shapes=[pltpu.VMEM((tm, d), BF16)],
        compiler_params=pltpu.CompilerParams(
            dimension_semantics=("arbitrary", "arbitrary"),
            vmem_limit_bytes=V7X_VMEM_LIMIT_BYTES),
        name="ffn",
    )(x2d, g_pre.reshape(1, d), g_post.reshape(1, d), w_gate, w_up, w_down)


def _pair_layout():
    half = ROT_DIM // 2
    perm, member = [], []
    for slab in range(2):
        for lane in range(SUB_HEAD_DIM):
            if lane < ROT_DIM:
                sub, dim = lane // half, lane % half + slab * half
            else:
                sub, dim = slab, lane
            perm.append(sub * SUB_HEAD_DIM + dim)
            member.append(sub)
    return perm, member


def _qkv_kernel(x_ref, gk_ref, gq_ref, w_ref, cos_ref, sin_ref, k_ref, v_ref, q_ref,
                hk_ref, hq_ref, *, nk, nv, q_scale, tn, tile):
    j = pl.program_id(1)
    groups = [slice(r, r + tile) for r in range(0, x_ref.shape[0], tile)]

    def pre_norm(rows):
        x = x_ref[rows, :]
        xn = x * _rms_scale(x)
        hk_ref[rows, :] = (xn * gk_ref[...]).astype(BF16)
        hq_ref[rows, :] = (xn * gq_ref[...]).astype(BF16)

    def project(h_ref, out_ref, rope, scale, transposed):
        for ti, rows in enumerate(groups):
            for c0 in range(0, tn, 2 * SUB_HEAD_DIM):
                y = jnp.dot(h_ref[rows, :], w_ref[:, c0:c0 + 2 * SUB_HEAD_DIM], preferred_element_type=F32)
                a, b = y[:, :SUB_HEAD_DIM], y[:, SUB_HEAD_DIM:]
                if rope:
                    cos, sin = cos_ref[rows, :], sin_ref[rows, :]
                    a, b = a * cos - b * sin, b * cos + a * sin
                if scale is not None:
                    a, b = a * scale, b * scale
                if transposed:
                    out_ref[ti, c0:c0 + SUB_HEAD_DIM, :] = a.T.astype(BF16)
                    out_ref[ti, c0 + SUB_HEAD_DIM:c0 + 2 * SUB_HEAD_DIM, :] = b.T.astype(BF16)
                else:
                    out_ref[rows, c0:c0 + SUB_HEAD_DIM] = a.astype(BF16)
                    out_ref[rows, c0 + SUB_HEAD_DIM:c0 + 2 * SUB_HEAD_DIM] = b.astype(BF16)

    @pl.when(j == 0)
    def _():
        for rows in groups:
            pre_norm(rows)
        project(hk_ref, k_ref, True, None, False)

    @pl.when((j > 0) & (j < nk))
    def _():
        project(hk_ref, k_ref, True, None, False)

    @pl.when((j >= nk) & (j < nk + nv))
    def _():
        project(hk_ref, v_ref, False, None, True)

    @pl.when(j >= nk + nv)
    def _():
        project(hq_ref, q_ref, True, q_scale, True)


def _qkv_proj(x2d, g_kv, g_q, w_cat, rope_cos, rope_sin, *, d_k, d_v, d_q, q_scale, tile, tm=1024, tn=1024):
    t, d = x2d.shape
    s = rope_cos.shape[0]
    assert t % tm == 0 and s % tm == 0 and d_k % tn == 0 and d_v % tn == 0 and d_q % tn == 0 and tm % tile == 0
    nk, nv, nq = d_k // tn, d_v // tn, d_q // tn
    ns = s // tm
    vec = pl.BlockSpec((1, d), lambda i, j: (0, 0))
    tab = pl.BlockSpec((tm, SUB_HEAD_DIM), lambda i, j: (i % ns, 0))
    k_spec = pl.BlockSpec((tm, tn), lambda i, j: (i, jnp.clip(j, 0, nk - 1)))
    v_spec = pl.BlockSpec((tm // tile, tn, tile), lambda i, j: (i, jnp.clip(j - nk, 0, nv - 1), 0))
    q_spec = pl.BlockSpec((tm // tile, tn, tile), lambda i, j: (i, jnp.clip(j - nk - nv, 0, nq - 1), 0))
    tiled = lambda n: jax.ShapeDtypeStruct((t // tile, n, tile), BF16)
    return pl.pallas_call(
        functools.partial(_qkv_kernel, nk=nk, nv=nv, q_scale=q_scale, tn=tn, tile=tile),
        out_shape=(jax.ShapeDtypeStruct((t, d_k), BF16), tiled(d_v), tiled(d_q)),
        grid=(t // tm, nk + nv + nq),
        in_specs=[
            pl.BlockSpec((tm, d), lambda i, j: (i, 0)),
            vec, vec,
            pl.BlockSpec((d, tn), lambda i, j: (0, j)),
            tab, tab,
        ],
        out_specs=(k_spec, v_spec, q_spec),
        scratch_shapes=[pltpu.VMEM((tm, d), BF16), pltpu.VMEM((tm, d), BF16)],
        compiler_params=pltpu.CompilerParams(
            dimension_semantics=("arbitrary", "arbitrary"),
            vmem_limit_bytes=V7X_VMEM_LIMIT_BYTES),
        name="qkv_proj",
    )(x2d, g_kv.reshape(1, d), g_q.reshape(1, d), w_cat, rope_cos, rope_sin)


def _attn_kernel(q_ref, k_ref, v_ref, member_ref, lq1_ref, lk1_ref, lq2_ref, lk2_ref, g_ref, o_ref,
                 acc_ref, s_ref, *, tq, nq, lambda_init):
    member = member_ref[...]
    lam = (jnp.exp(jnp.sum(lq1_ref[...] * lk1_ref[...], axis=-1, keepdims=True))
           - jnp.exp(jnp.sum(lq2_ref[...] * lk2_ref[...], axis=-1, keepdims=True))
           + lambda_init)

    def q_tile(qi):
        base = qi % 2
        q = q_ref[0, qi]
        qs = [jnp.where(member == i, q, jnp.zeros_like(q)) for i in range(2)]
        acc_ref[base] = jnp.zeros_like(acc_ref[base])

        def scores(ki, slot, masked):
            bmax = []
            k = k_ref[0, pl.ds(ki * tq, tq), :]
            for i in range(2):
                s = jnp.dot(k, qs[i], preferred_element_type=F32)
                if masked:
                    kpos = lax.broadcasted_iota(jnp.int32, (tq, tq), 0)
                    qpos = lax.broadcasted_iota(jnp.int32, (tq, tq), 1)
                    s = jnp.where(kpos <= qpos, s, -jnp.inf)
                s_ref[slot, i] = s
                bmax.append(jnp.max(s, axis=0, keepdims=True))
            return tuple(bmax)

        def accumulate(ki, slot, state):
            carry, bmax = state[:4], state[4:]
            v = v_ref[0, ki]
            new = []
            for i in range(2):
                m, l = carry[2 * i], carry[2 * i + 1]
                m_new = jnp.maximum(m, bmax[i])
                alpha = jnp.exp2(m - m_new)
                p = jnp.exp2(s_ref[slot, i] - m_new)
                l = alpha * l + jnp.sum(p, axis=0, keepdims=True)
                pv = jnp.dot(v, p.astype(BF16), preferred_element_type=F32)
                acc_ref[base, i] = alpha * acc_ref[base, i] + pv
                new += [m_new, l]
            return tuple(new)

        def before(ki):
            return jnp.where(ki == 0, qi, ki - 1)

        def pair(j, state):
            ki = qi % 2 + 2 * j
            state = accumulate(before(ki), base, state) + scores(ki, 1 - base, False)
            return accumulate(ki, 1 - base, state) + scores(ki + 1, base, False)

        m0 = jnp.full((1, tq), -jnp.inf, F32)
        l0 = jnp.zeros((1, tq), F32)
        state = (m0, l0, m0, l0) + scores(qi, base, True)
        if qi % 2 == 1:
            state = accumulate(qi, base, state) + scores(0, base, False)
        if qi // 2 > 0:
            state = lax.fori_loop(0, qi // 2, pair, state)
        _, l1, _, l2 = accumulate(max(qi - 1, 0), base, state)

        ot = acc_ref[base, 0] * (1.0 / l1) - lam * (acc_ref[base, 1] * (1.0 / l2))
        ot = ot * lax.rsqrt(jnp.mean(ot * ot, axis=0, keepdims=True) + EPS)
        o_ref[0, qi * tq:(qi + 1) * tq, :] = ((ot.T * g_ref[...]) * (1.0 - lambda_init)).astype(o_ref.dtype)

    for qi in range(nq):
        q_tile(qi)


def _diff_attn(q, k, v, lq1, lk1, lq2, lk2, subln_gain, lambda_init, *, tq=512):
    b, s, d = k.shape
    nh = d // V_HEAD_DIM
    nq = s // tq
    assert q.shape == v.shape == (b, nq, d, tq)
    lvec = pl.BlockSpec((1, SUB_HEAD_DIM), lambda bi, hi: (0, 0))
    hvec = pl.BlockSpec((1, V_HEAD_DIM), lambda bi, hi: (0, 0))
    member = jnp.asarray(_pair_layout()[1], jnp.int32).reshape(V_HEAD_DIM, 1)
    seq = pl.BlockSpec((1, s, V_HEAD_DIM), lambda bi, hi: (bi, 0, hi))
    tiles = pl.BlockSpec((1, nq, V_HEAD_DIM, tq), lambda bi, hi: (bi, 0, hi, 0))
    return pl.pallas_call(
        functools.partial(_attn_kernel, tq=tq, nq=nq, lambda_init=lambda_init),
        out_shape=jax.ShapeDtypeStruct((b, s, d), BF16),
        grid=(b, nh),
        in_specs=[tiles, seq, tiles, pl.BlockSpec((V_HEAD_DIM, 1), lambda bi, hi: (0, 0)),
                  lvec, lvec, lvec, lvec, hvec],
        out_specs=seq,
        scratch_shapes=[pltpu.VMEM((2, 2, V_HEAD_DIM, tq), F32), pltpu.VMEM((2, 2, tq, tq), F32)],
        compiler_params=pltpu.CompilerParams(
            dimension_semantics=("arbitrary", "arbitrary"),
            vmem_limit_bytes=V7X_VMEM_LIMIT_BYTES),
        name="diff_attn",
    )(q, k, v, member, lq1.reshape(1, -1), lk1.reshape(1, -1), lq2.reshape(1, -1), lk2.reshape(1, -1),
      subln_gain.reshape(1, -1))


def _oproj_kernel(a_ref, x_ref, w_ref, g_ref, o_ref):
    tm = x_ref.shape[0]
    for rows in (slice(0, tm // 2), slice(tm // 2, tm)):
        m = jnp.dot(a_ref[rows, :], w_ref[...], preferred_element_type=F32)
        o_ref[rows, :] = x_ref[rows, :] + (m * _rms_scale(m)) * g_ref[...]


def _out_proj(a2d, x2d, w_o, g_post, *, tm=512):
    t, d = x2d.shape
    assert t % tm == 0
    row = pl.BlockSpec((tm, d), lambda i: (i, 0))
    return pl.pallas_call(
        _oproj_kernel,
        out_shape=jax.ShapeDtypeStruct(x2d.shape, x2d.dtype),
        grid=(t // tm,),
        in_specs=[row, row,
                  pl.BlockSpec((d, d), lambda i: (0, 0)),
                  pl.BlockSpec((1, d), lambda i: (0, 0))],
        out_specs=row,
        compiler_params=pltpu.CompilerParams(
            dimension_semantics=("arbitrary",),
            vmem_limit_bytes=V7X_VMEM_LIMIT_BYTES),
        name="out_proj",
    )(a2d, x2d, w_o, g_post.reshape(1, d))


def _rope_tables(s):
    inv = ROPE_THETA ** (-jnp.arange(0, ROT_DIM, 2, dtype=F32) / ROT_DIM)
    ang = jnp.arange(s, dtype=F32)[:, None] * inv[None, :]
    cos, sin = jnp.cos(ang), jnp.sin(ang)
    rest = SUB_HEAD_DIM - ROT_DIM
    cos = jnp.concatenate([cos, cos, jnp.ones((s, rest), F32)], axis=-1)
    sin = jnp.concatenate([sin, sin, jnp.zeros((s, rest), F32)], axis=-1)
    return cos, sin


def _wprep_kernel(wkv_ref, wq_ref, o_ref, *, nk, nv, tn):
    j = pl.program_id(0)
    half = ROT_DIM // 2

    def pair_layout(w):
        lane = lax.broadcasted_iota(jnp.int32, (1, SUB_HEAD_DIM), 1)
        for c0 in range(0, tn, V_HEAD_DIM):
            a = w[:, c0:c0 + SUB_HEAD_DIM]
            b = w[:, c0 + SUB_HEAD_DIM:c0 + V_HEAD_DIM]
            s0 = jnp.where((lane >= half) & (lane < ROT_DIM), pltpu.roll(b, half, axis=1), a)
            s1 = jnp.where(lane < half, pltpu.roll(a, SUB_HEAD_DIM - half, axis=1), b)
            o_ref[:, c0:c0 + SUB_HEAD_DIM] = s0.astype(BF16)
            o_ref[:, c0 + SUB_HEAD_DIM:c0 + V_HEAD_DIM] = s1.astype(BF16)

    @pl.when(j < nk)
    def _():
        pair_layout(wkv_ref[...])

    @pl.when((j >= nk) & (j < nk + nv))
    def _():
        o_ref[...] = wkv_ref[...].astype(BF16)

    @pl.when(j >= nk + nv)
    def _():
        pair_layout(wq_ref[...])


def _prep_qkv_weights(w_kv, w_q, *, d_k, tn=512):
    d_in, d_kv = w_kv.shape
    d_q = w_q.shape[1]
    assert d_k % tn == 0 and d_kv % tn == 0 and d_q % tn == 0 and tn % V_HEAD_DIM == 0
    nk, nkv, nq = d_k // tn, d_kv // tn, d_q // tn
    return pl.pallas_call(
        functools.partial(_wprep_kernel, nk=nk, nv=nkv - nk, tn=tn),
        out_shape=jax.ShapeDtypeStruct((d_in, d_kv + d_q), BF16),
        grid=(nkv + nq,),
        in_specs=[pl.BlockSpec((d_in, tn), lambda j: (0, jnp.minimum(j, nkv - 1))),
                  pl.BlockSpec((d_in, tn), lambda j: (0, jnp.maximum(j - nkv, 0)))],
        out_specs=pl.BlockSpec((d_in, tn), lambda j: (0, j)),
        compiler_params=pltpu.CompilerParams(
            dimension_semantics=("arbitrary",),
            vmem_limit_bytes=V7X_VMEM_LIMIT_BYTES),
        name="qkv_weight_prep",
    )(w_kv, w_q)


def _lambda_init_for(layer_idx):
    return 0.8 - 0.6 * math.exp(-0.3 * layer_idx)


def kernel(x, norm_mix_pre, norm_mix_post, norm_ffn_pre, norm_ffn_post, w_pool, pool_scale, kv_norm, w_kv, w_q, lambda_q1, lambda_k1, lambda_q2, lambda_k2, subln_gain, w_o, w_ffn_gate, w_ffn_up, w_ffn_down):
    b, s, d = x.shape
    depth = norm_mix_pre.shape[0]
    n_a = w_pool.shape[0]
    rope_cos, rope_sin = _rope_tables(s)
    q_scale = SUB_HEAD_DIM ** -0.5 * LOG2E
    w_gate, w_up, w_down = w_ffn_gate.astype(BF16), w_ffn_up.astype(BF16), w_ffn_down.astype(BF16)
    k_sh = v_sh = None
    for l in range(depth):
        if l < n_a:
            x = _pool_mixer(x, norm_mix_pre[l], norm_mix_post[l], w_pool[l].astype(BF16), pool_scale[l])
        else:
            jb = l - n_a
            x2d = x.reshape(b * s, d)
            w_cat = _prep_qkv_weights(w_kv, w_q[jb], d_k=d)
            k_l, v_l, q = _qkv_proj(x2d, kv_norm, norm_mix_pre[l], w_cat, rope_cos, rope_sin,
                                    d_k=d, d_v=d, d_q=d, q_scale=q_scale, tile=ATTN_TILE)
            tiled = (b, s // ATTN_TILE, d, ATTN_TILE)
            if k_sh is None:
                k_sh, v_sh = k_l.reshape(b, s, d), v_l.reshape(tiled)
            a = _diff_attn(q.reshape(tiled), k_sh, v_sh, lambda_q1[jb], lambda_k1[jb], lambda_q2[jb],
                           lambda_k2[jb], subln_gain[jb], _lambda_init_for(l), tq=ATTN_TILE)
            x = _out_proj(a.reshape(b * s, d), x2d, w_o[jb].astype(BF16), norm_mix_post[l]).reshape(b, s, d)
        x = _ffn(x.reshape(b * s, d), norm_ffn_pre[l], norm_ffn_post[l], w_gate, w_up, w_down, l).reshape(b, s, d)
    return x
```

```python
import functools
import math

import jax
import jax.numpy as jnp
from jax import lax
from jax.experimental import pallas as pl
from jax.experimental.pallas import tpu as pltpu

F32 = jnp.float32
BF16 = jnp.bfloat16

EPS = 1e-6
POOL_WINDOWS = (2, 4, 8, 16)
POOL_HALO = 16
SUB_HEAD_DIM = 128
V_HEAD_DIM = 2 * SUB_HEAD_DIM
ROT_DIM = SUB_HEAD_DIM // 4
ROPE_THETA = 500000.0
LOG2E = 1.4426950408889634
ATTN_TILE = 512

V7X_VMEM_LIMIT_BYTES = 56 * 1024 * 1024


def _rms_scale(x):
    return lax.rsqrt(jnp.mean(x * x, axis=-1, keepdims=True) + EPS)


def _pool_kernel(x_ref, gpre_ref, gpost_ref, w_ref, scale_ref, o_ref, halo_ref, *, ts, gc):
    si = pl.program_id(1)

    @pl.when(si == 0)
    def _():
        halo_ref[...] = jnp.zeros_like(halo_ref)

    x = x_ref[0]
    h = (x * _rms_scale(x)) * gpre_ref[...]
    pos = si * ts + lax.broadcasted_iota(jnp.int32, (ts, 1), 0)
    outs = []
    for g, w in enumerate(POOL_WINDOWS):
        hg = h[:, g * gc:(g + 1) * gc]
        s = jnp.concatenate([halo_ref[:, g * gc:(g + 1) * gc], hg], axis=0)
        k = 1
        while k < w:
            s = s + pltpu.roll(s, k, axis=0)
            k *= 2
        cnt = jnp.minimum(pos + 1, w).astype(F32)
        mixed = s[POOL_HALO:, :] * (1.0 / cnt) - hg
        outs.append(jnp.dot(mixed.astype(BF16), w_ref[g], preferred_element_type=F32))
    halo_ref[...] = h[ts - POOL_HALO:, :]
    m = jnp.concatenate(outs, axis=-1) * scale_ref[...]
    o_ref[0] = x + (m * _rms_scale(m)) * gpost_ref[...]


def _pool_mixer(x, g_pre, g_post, w_pool, pool_scale, *, ts=512):
    b, s, d = x.shape
    ng, gc, _ = w_pool.shape
    assert ng == len(POOL_WINDOWS) and ng * gc == d and s % ts == 0
    vec = pl.BlockSpec((1, d), lambda bi, si: (0, 0))
    return pl.pallas_call(
        functools.partial(_pool_kernel, ts=ts, gc=gc),
        out_shape=jax.ShapeDtypeStruct(x.shape, x.dtype),
        grid=(b, s // ts),
        in_specs=[
            pl.BlockSpec((1, ts, d), lambda bi, si: (bi, si, 0)),
            vec, vec,
            pl.BlockSpec((ng, gc, gc), lambda bi, si: (0, 0, 0)),
            vec,
        ],
        out_specs=pl.BlockSpec((1, ts, d), lambda bi, si: (bi, si, 0)),
        scratch_shapes=[pltpu.VMEM((POOL_HALO, d), F32)],
        compiler_params=pltpu.CompilerParams(
            dimension_semantics=("arbitrary", "arbitrary"),
            vmem_limit_bytes=V7X_VMEM_LIMIT_BYTES),
        name="pool_mixer",
    )(x, g_pre.reshape(1, d), g_post.reshape(1, d), w_pool, pool_scale.reshape(1, d))


def _ffn_kernel(x_ref, gpre_ref, gpost_ref, wg_ref, wu_ref, wd_ref, o_ref, h_ref, *, nj):
    j = pl.program_id(1)
    tm = x_ref.shape[0]
    whole = (slice(0, tm),)
    halves = (slice(0, tm // 2), slice(tm // 2, tm))

    def pre_norm(rows):
        x = x_ref[rows, :]
        h_ref[rows, :] = ((x * _rms_scale(x)) * gpre_ref[...]).astype(BF16)

    def swiglu(rows, first):
        h = h_ref[rows, :]
        gate = jnp.dot(h, wg_ref[...], preferred_element_type=F32)
        up = jnp.dot(h, wu_ref[...], preferred_element_type=F32)
        act = (gate * jax.nn.sigmoid(gate)) * up
        down = jnp.dot(act.astype(BF16), wd_ref[...], preferred_element_type=F32)
        o_ref[rows, :] = down if first else o_ref[rows, :] + down

    def finish(rows):
        m = o_ref[rows, :]
        o_ref[rows, :] = x_ref[rows, :] + (m * _rms_scale(m)) * gpost_ref[...]

    @pl.when(j == 0)
    def _():
        for rows in halves:
            pre_norm(rows)
        for rows in halves:
            swiglu(rows, True)

    @pl.when((j > 0) & (j < nj - 1))
    def _():
        for rows in whole:
            swiglu(rows, False)

    @pl.when(j == nj - 1)
    def _():
        for rows in halves:
            swiglu(rows, False)
        for rows in halves:
            finish(rows)


def _ffn(x2d, g_pre, g_post, w_gate, w_up, w_down, layer, *, tm=1024, tf=512):
    t, d = x2d.shape
    dff = w_gate.shape[2]
    assert t % tm == 0 and dff % tf == 0 and dff // tf >= 2
    nj = dff // tf
    vec = pl.BlockSpec((1, d), lambda i, j: (0, 0))
    return pl.pallas_call(
        functools.partial(_ffn_kernel, nj=nj),
        out_shape=jax.ShapeDtypeStruct(x2d.shape, x2d.dtype),
        grid=(t // tm, nj),
        in_specs=[
            pl.BlockSpec((tm, d), lambda i, j: (i, 0)),
            vec, vec,
            pl.BlockSpec((None, d, tf), lambda i, j: (layer, 0, j)),
            pl.BlockSpec((None, d, tf), lambda i, j: (layer, 0, j)),
            pl.BlockSpec((None, tf, d), lambda i, j: (layer, j, 0)),
        ],
        out_specs=pl.BlockSpec((tm, d), lambda i, j: (i, 0)),
        scratch_shapes=[pltpu.VMEM((tm, d), BF16)],
        compiler_params=pltpu.CompilerParams(
            dimension_semantics=("arbitrary", "arbitrary"),
            vmem_limit_bytes=V7X_VMEM_LIMIT_BYTES),
        name="ffn",
    )(x2d, g_pre.reshape(1, d), g_post.reshape(1, d), w_gate, w_up, w_down)


def _pair_layout():
    half = ROT_DIM // 2
    perm, member = [], []
    for slab in range(2):
        for lane in range(SUB_HEAD_DIM):
            if lane < ROT_DIM:
                sub, dim = lane // half, lane % half + slab * half
            else:
                sub, dim = slab, lane
            perm.append(sub * SUB_HEAD_DIM + dim)
            member.append(sub)
    return perm, member


def _qkv_kernel(x_ref, gk_ref, gq_ref, w_ref, cos_ref, sin_ref, k_ref, v_ref, q_ref,
                hk_ref, hq_ref, *, nk, nv, q_scale, tn, tile):
    j = pl.program_id(1)
    groups = [slice(r, r + tile) for r in range(0, x_ref.shape[0], tile)]

    def pre_norm(rows):
        x = x_ref[rows, :]
        xn = x * _rms_scale(x)
        hk_ref[rows, :] = (xn * gk_ref[...]).astype(BF16)
        hq_ref[rows, :] = (xn * gq_ref[...]).astype(BF16)

    def project(h_ref, out_ref, rope, scale, transposed):
        for ti, rows in enumerate(groups):
            for c0 in range(0, tn, 2 * SUB_HEAD_DIM):
                y = jnp.dot(h_ref[rows, :], w_ref[:, c0:c0 + 2 * SUB_HEAD_DIM], preferred_element_type=F32)
                a, b = y[:, :SUB_HEAD_DIM], y[:, SUB_HEAD_DIM:]
                if rope:
                    cos, sin = cos_ref[rows, :], sin_ref[rows, :]
                    a, b = a * cos - b * sin, b * cos + a * sin
                if scale is not None:
                    a, b = a * scale, b * scale
                if transposed:
                    out_ref[ti, c0:c0 + SUB_HEAD_DIM, :] = a.T.astype(BF16)
                    out_ref[ti, c0 + SUB_HEAD_DIM:c0 + 2 * SUB_HEAD_DIM, :] = b.T.astype(BF16)
                else:
                    out_ref[rows, c0:c0 + SUB_HEAD_DIM] = a.astype(BF16)
                    out_ref[rows, c0 + SUB_HEAD_DIM:c0 + 2 * SUB_HEAD_DIM] = b.astype(BF16)

    @pl.when(j == 0)
    def _():
        for rows in groups:
            pre_norm(rows)
        project(hk_ref, k_ref, True, None, False)

    @pl.when((j > 0) & (j < nk))
    def _():
        project(hk_ref, k_ref, True, None, False)

    @pl.when((j >= nk) & (j < nk + nv))
    def _():
        project(hk_ref, v_ref, False, None, True)

    @pl.when(j >= nk + nv)
    def _():
        project(hq_ref, q_ref, True, q_scale, True)


def _qkv_proj(x2d, g_kv, g_q, w_cat, rope_cos, rope_sin, *, d_k, d_v, d_q, q_scale, tile, tm=1024, tn=1024):
    t, d = x2d.shape
    s = rope_cos.shape[0]
    assert t % tm == 0 and s % tm == 0 and d_k % tn == 0 and d_v % tn == 0 and d_q % tn == 0 and tm % tile == 0
    nk, nv, nq = d_k // tn, d_v // tn, d_q // tn
    ns = s // tm
    vec = pl.BlockSpec((1, d), lambda i, j: (0, 0))
    tab = pl.BlockSpec((tm, SUB_HEAD_DIM), lambda i, j: (i % ns, 0))
    k_spec = pl.BlockSpec((tm, tn), lambda i, j: (i, jnp.clip(j, 0, nk - 1)))
    v_spec = pl.BlockSpec((tm // tile, tn, tile), lambda i, j: (i, jnp.clip(j - nk, 0, nv - 1), 0))
    q_spec = pl.BlockSpec((tm // tile, tn, tile), lambda i, j: (i, jnp.clip(j - nk - nv, 0, nq - 1), 0))
    tiled = lambda n: jax.ShapeDtypeStruct((t // tile, n, tile), BF16)
    return pl.pallas_call(
        functools.partial(_qkv_kernel, nk=nk, nv=nv, q_scale=q_scale, tn=tn, tile=tile),
        out_shape=(jax.ShapeDtypeStruct((t, d_k), BF16), tiled(d_v), tiled(d_q)),
        grid=(t // tm, nk + nv + nq),
        in_specs=[
            pl.BlockSpec((tm, d), lambda i, j: (i, 0)),
            vec, vec,
            pl.BlockSpec((d, tn), lambda i, j: (0, j)),
            tab, tab,
        ],
        out_specs=(k_spec, v_spec, q_spec),
        scratch_shapes=[pltpu.VMEM((tm, d), BF16), pltpu.VMEM((tm, d), BF16)],
        compiler_params=pltpu.CompilerParams(
            dimension_semantics=("arbitrary", "arbitrary"),
            vmem_limit_bytes=V7X_VMEM_LIMIT_BYTES),
        name="qkv_proj",
    )(x2d, g_kv.reshape(1, d), g_q.reshape(1, d), w_cat, rope_cos, rope_sin)


def _attn_kernel(q_ref, k_ref, v_ref, member_ref, lq1_ref, lk1_ref, lq2_ref, lk2_ref, g_ref, o_ref,
                 acc_ref, s_ref, *, tq, nq, lambda_init):
    member = member_ref[...]
    lam = (jnp.exp(jnp.sum(lq1_ref[...] * lk1_ref[...], axis=-1, keepdims=True))
           - jnp.exp(jnp.sum(lq2_ref[...] * lk2_ref[...], axis=-1, keepdims=True))
           + lambda_init)

    def q_tile(qi, diag_slot):
        base = qi % 2
        half = tq // 2
        q = q_ref[0, qi]
        qs = [jnp.where(member == i, q, jnp.zeros_like(q)) for i in range(2)]

        def causal(s):
            kpos = lax.broadcasted_iota(jnp.int32, s.shape, 0)
            qpos = lax.broadcasted_iota(jnp.int32, s.shape, 1)
            return jnp.where(kpos <= qpos, s, -jnp.inf)

        def diag_scores(slot):
            bmax = []
            k = k_ref[0, qi * tq:(qi + 1) * tq, :]
            for i in range(2):
                s_lo = jnp.dot(k[:half], qs[i], preferred_element_type=F32)
                s_lo = jnp.concatenate([causal(s_lo[:, :half]), s_lo[:, half:]], axis=1)
                s_hi = causal(jnp.dot(k[half:], qs[i][:, half:], preferred_element_type=F32))
                s_ref[slot, i, :half, :] = s_lo
                s_ref[slot, i, half:, half:] = s_hi
                b_lo = jnp.max(s_lo, axis=0, keepdims=True)
                b_hi = jnp.max(s_hi, axis=0, keepdims=True)
                bmax.append(jnp.concatenate([b_lo[:, :half], jnp.maximum(b_lo[:, half:], b_hi)], axis=1))
            return tuple(bmax)

        def diag_accumulate(slot, bmax):
            v = v_ref[0, qi]
            new = []
            for i in range(2):
                m_new = bmax[i]
                p_lo = jnp.exp2(s_ref[slot, i, :half, :] - m_new)
                p_hi = jnp.exp2(s_ref[slot, i, half:, half:] - m_new[:, half:])
                l_hi = jnp.sum(p_hi, axis=0, keepdims=True)
                l = jnp.sum(p_lo, axis=0, keepdims=True) + jnp.concatenate([jnp.zeros_like(l_hi), l_hi], axis=1)
                pv = jnp.dot(v[:, :half], p_lo.astype(BF16), preferred_element_type=F32)
                pv_hi = jnp.dot(v[:, half:], p_hi.astype(BF16), preferred_element_type=F32)
                acc_ref[base, i, :, :half] = pv[:, :half]
                acc_ref[base, i, :, half:] = pv[:, half:] + pv_hi
                new += [m_new, l]
            return tuple(new)

        def scores(ki, slot):
            bmax = []
            k = k_ref[0, pl.ds(ki * tq, tq), :]
            for i in range(2):
                s = jnp.dot(k, qs[i], preferred_element_type=F32)
                s_ref[slot, i] = s
                bmax.append(jnp.max(s, axis=0, keepdims=True))
            return tuple(bmax)

        def accumulate(ki, slot, state):
            carry, bmax = state[:4], state[4:]
            v = v_ref[0, ki]
            new = []
            for i in range(2):
                m, l = carry[2 * i], carry[2 * i + 1]
                m_new = jnp.maximum(m, bmax[i])
                alpha = jnp.exp2(m - m_new)
                p = jnp.exp2(s_ref[slot, i] - m_new)
                l = alpha * l + jnp.sum(p, axis=0, keepdims=True)
                pv = jnp.dot(v, p.astype(BF16), preferred_element_type=F32)
                acc_ref[base, i] = alpha * acc_ref[base, i] + pv
                new += [m_new, l]
            return tuple(new)

        bmax = diag_scores(diag_slot)
        slot = diag_slot
        if qi == 0:
            _, l1, _, l2 = diag_accumulate(slot, bmax)
        else:
            slot = 1 - diag_slot
            state = diag_accumulate(diag_slot, bmax) + scores(0, slot)
            first = 0
            if (qi - 1) % 2 == 1:
                state = accumulate(0, slot, state) + scores(1, diag_slot)
                first, slot = 1, diag_slot

            def pair(j, state):
                ki = first + 2 * j
                state = accumulate(ki, slot, state) + scores(ki + 1, 1 - slot)
                return accumulate(ki + 1, 1 - slot, state) + scores(ki + 2, slot)

            if (qi - 1) // 2 > 0:
                state = lax.fori_loop(0, (qi - 1) // 2, pair, state)
            _, l1, _, l2 = accumulate(qi - 1, slot, state)

        ot = acc_ref[base, 0] * (1.0 / l1) - lam * (acc_ref[base, 1] * (1.0 / l2))
        ot = ot * lax.rsqrt(jnp.mean(ot * ot, axis=0, keepdims=True) + EPS)
        o_ref[0, qi * tq:(qi + 1) * tq, :] = ((ot.T * g_ref[...]) * (1.0 - lambda_init)).astype(o_ref.dtype)
        return slot

    last_slot = 1
    for qi in range(nq):
        last_slot = q_tile(qi, 1 - last_slot)


def _diff_attn(q, k, v, lq1, lk1, lq2, lk2, subln_gain, lambda_init, *, tq=512):
    b, s, d = k.shape
    nh = d // V_HEAD_DIM
    nq = s // tq
    assert q.shape == v.shape == (b, nq, d, tq)
    lvec = pl.BlockSpec((1, SUB_HEAD_DIM), lambda bi, hi: (0, 0))
    hvec = pl.BlockSpec((1, V_HEAD_DIM), lambda bi, hi: (0, 0))
    member = jnp.asarray(_pair_layout()[1], jnp.int32).reshape(V_HEAD_DIM, 1)
    seq = pl.BlockSpec((1, s, V_HEAD_DIM), lambda bi, hi: (bi, 0, hi))
    tiles = pl.BlockSpec((1, nq, V_HEAD_DIM, tq), lambda bi, hi: (bi, 0, hi, 0))
    return pl.pallas_call(
        functools.partial(_attn_kernel, tq=tq, nq=nq, lambda_init=lambda_init),
        out_shape=jax.ShapeDtypeStruct((b, s, d), BF16),
        grid=(b, nh),
        in_specs=[tiles, seq, tiles, pl.BlockSpec((V_HEAD_DIM, 1), lambda bi, hi: (0, 0)),
                  lvec, lvec, lvec, lvec, hvec],
        out_specs=seq,
        scratch_shapes=[pltpu.VMEM((2, 2, V_HEAD_DIM, tq), F32), pltpu.VMEM((2, 2, tq, tq), F32)],
        compiler_params=pltpu.CompilerParams(
            dimension_semantics=("arbitrary", "arbitrary"),
            vmem_limit_bytes=V7X_VMEM_LIMIT_BYTES),
        name="diff_attn",
    )(q, k, v, member, lq1.reshape(1, -1), lk1.reshape(1, -1), lq2.reshape(1, -1), lk2.reshape(1, -1),
      subln_gain.reshape(1, -1))


def _oproj_kernel(a_ref, x_ref, w_ref, g_ref, o_ref):
    tm = x_ref.shape[0]
    for rows in (slice(0, tm // 2), slice(tm // 2, tm)):
        m = jnp.dot(a_ref[rows, :], w_ref[...], preferred_element_type=F32)
        o_ref[rows, :] = x_ref[rows, :] + (m * _rms_scale(m)) * g_ref[...]


def _out_proj(a2d, x2d, w_o, g_post, *, tm=512):
    t, d = x2d.shape
    assert t % tm == 0
    row = pl.BlockSpec((tm, d), lambda i: (i, 0))
    return pl.pallas_call(
        _oproj_kernel,
        out_shape=jax.ShapeDtypeStruct(x2d.shape, x2d.dtype),
        grid=(t // tm,),
        in_specs=[row, row,
                  pl.BlockSpec((d, d), lambda i: (0, 0)),
                  pl.BlockSpec((1, d), lambda i: (0, 0))],
        out_specs=row,
        compiler_params=pltpu.CompilerParams(
            dimension_semantics=("arbitrary",),
            vmem_limit_bytes=V7X_VMEM_LIMIT_BYTES),
        name="out_proj",
    )(a2d, x2d, w_o, g_post.reshape(1, d))


def _rope_tables(s):
    inv = ROPE_THETA ** (-jnp.arange(0, ROT_DIM, 2, dtype=F32) / ROT_DIM)
    ang = jnp.arange(s, dtype=F32)[:, None] * inv[None, :]
    cos, sin = jnp.cos(ang), jnp.sin(ang)
    rest = SUB_HEAD_DIM - ROT_DIM
    cos = jnp.concatenate([cos, cos, jnp.ones((s, rest), F32)], axis=-1)
    sin = jnp.concatenate([sin, sin, jnp.zeros((s, rest), F32)], axis=-1)
    return cos, sin


def _wprep_kernel(wkv_ref, wq_ref, o_ref, *, nk, nv, tn):
    j = pl.program_id(0)
    half = ROT_DIM // 2

    def pair_layout(w):
        lane = lax.broadcasted_iota(jnp.int32, (1, SUB_HEAD_DIM), 1)
        for c0 in range(0, tn, V_HEAD_DIM):
            a = w[:, c0:c0 + SUB_HEAD_DIM]
            b = w[:, c0 + SUB_HEAD_DIM:c0 + V_HEAD_DIM]
            s0 = jnp.where((lane >= half) & (lane < ROT_DIM), pltpu.roll(b, half, axis=1), a)
            s1 = jnp.where(lane < half, pltpu.roll(a, SUB_HEAD_DIM - half, axis=1), b)
            o_ref[:, c0:c0 + SUB_HEAD_DIM] = s0.astype(BF16)
            o_ref[:, c0 + SUB_HEAD_DIM:c0 + V_HEAD_DIM] = s1.astype(BF16)

    @pl.when(j < nk)
    def _():
        pair_layout(wkv_ref[...])

    @pl.when((j >= nk) & (j < nk + nv))
    def _():
        o_ref[...] = wkv_ref[...].astype(BF16)

    @pl.when(j >= nk + nv)
    def _():
        pair_layout(wq_ref[...])


def _prep_qkv_weights(w_kv, w_q, *, d_k, tn=512):
    d_in, d_kv = w_kv.shape
    d_q = w_q.shape[1]
    assert d_k % tn == 0 and d_kv % tn == 0 and d_q % tn == 0 and tn % V_HEAD_DIM == 0
    nk, nkv, nq = d_k // tn, d_kv // tn, d_q // tn
    return pl.pallas_call(
        functools.partial(_wprep_kernel, nk=nk, nv=nkv - nk, tn=tn),
        out_shape=jax.ShapeDtypeStruct((d_in, d_kv + d_q), BF16),
        grid=(nkv + nq,),
        in_specs=[pl.BlockSpec((d_in, tn), lambda j: (0, jnp.minimum(j, nkv - 1))),
                  pl.BlockSpec((d_in, tn), lambda j: (0, jnp.maximum(j - nkv, 0)))],
        out_specs=pl.BlockSpec((d_in, tn), lambda j: (0, j)),
        compiler_params=pltpu.CompilerParams(
            dimension_semantics=("arbitrary",),
            vmem_limit_bytes=V7X_VMEM_LIMIT_BYTES),
        name="qkv_weight_prep",
    )(w_kv, w_q)


def _lambda_init_for(layer_idx):
    return 0.8 - 0.6 * math.exp(-0.3 * layer_idx)


def kernel(x, norm_mix_pre, norm_mix_post, norm_ffn_pre, norm_ffn_post, w_pool, pool_scale, kv_norm, w_kv, w_q, lambda_q1, lambda_k1, lambda_q2, lambda_k2, subln_gain, w_o, w_ffn_gate, w_ffn_up, w_ffn_down):
    b, s, d = x.shape
    depth = norm_mix_pre.shape[0]
    n_a = w_pool.shape[0]
    rope_cos, rope_sin = _rope_tables(s)
    q_scale = SUB_HEAD_DIM ** -0.5 * LOG2E
    w_gate, w_up, w_down = w_ffn_gate.astype(BF16), w_ffn_up.astype(BF16), w_ffn_down.astype(BF16)
    k_sh = v_sh = None
    for l in range(depth):
        if l < n_a:
            x = _pool_mixer(x, norm_mix_pre[l], norm_mix_post[l], w_pool[l].astype(BF16), pool_scale[l])
        else:
            jb = l - n_a
            x2d = x.reshape(b * s, d)
            w_cat = _prep_qkv_weights(w_kv, w_q[jb], d_k=d)
            k_l, v_l, q = _qkv_proj(x2d, kv_norm, norm_mix_pre[l], w_cat, rope_cos, rope_sin,
                                    d_k=d, d_v=d, d_q=d, q_scale=q_scale, tile=ATTN_TILE)
            tiled = (b, s // ATTN_TILE, d, ATTN_TILE)
            if k_sh is None:
                k_sh, v_sh = k_l.reshape(b, s, d), v_l.reshape(tiled)
            a = _diff_attn(q.reshape(tiled), k_sh, v_sh, lambda_q1[jb], lambda_k1[jb], lambda_q2[jb],
                           lambda_k2[jb], subln_gain[jb], _lambda_init_for(l), tq=ATTN_TILE)
            x = _out_proj(a.reshape(b * s, d), x2d, w_o[jb].astype(BF16), norm_mix_post[l]).reshape(b, s, d)
        x = _ffn(x.reshape(b * s, d), norm_ffn_pre[l], norm_ffn_post[l], w_gate, w_up, w_down, l).reshape(b, s, d)
    return x
```

```python
import functools
import math

import jax
import jax.numpy as jnp
from jax import lax
from jax.experimental import pallas as pl
from jax.experimental.pallas import tpu as pltpu

F32 = jnp.float32
BF16 = jnp.bfloat16

EPS = 1e-6
POOL_WINDOWS = (2, 4, 8, 16)
POOL_HALO = 16
SUB_HEAD_DIM = 128
V_HEAD_DIM = 2 * SUB_HEAD_DIM
ROT_DIM = SUB_HEAD_DIM // 4
ROPE_THETA = 500000.0
LOG2E = 1.4426950408889634
ATTN_TILE = 512

V7X_VMEM_LIMIT_BYTES = 58 * 1024 * 1024


def _rms_scale(x):
    return lax.rsqrt(jnp.mean(x * x, axis=-1, keepdims=True) + EPS)


def _pool_kernel(x_ref, gpre_ref, gpost_ref, w_ref, scale_ref, o_ref, halo_ref, *, ts, gc):
    si = pl.program_id(1)

    @pl.when(si == 0)
    def _():
        halo_ref[...] = jnp.zeros_like(halo_ref)

    x = x_ref[0]
    h = (x * _rms_scale(x)) * gpre_ref[...]
    pos = si * ts + lax.broadcasted_iota(jnp.int32, (ts, 1), 0)
    outs = []
    for g, w in enumerate(POOL_WINDOWS):
        hg = h[:, g * gc:(g + 1) * gc]
        s = jnp.concatenate([halo_ref[:, g * gc:(g + 1) * gc], hg], axis=0)
        k = 1
        while k < w:
            s = s + pltpu.roll(s, k, axis=0)
            k *= 2
        cnt = jnp.minimum(pos + 1, w).astype(F32)
        mixed = s[POOL_HALO:, :] * (1.0 / cnt) - hg
        outs.append(jnp.dot(mixed.astype(BF16), w_ref[g], preferred_element_type=F32))
    halo_ref[...] = h[ts - POOL_HALO:, :]
    m = jnp.concatenate(outs, axis=-1) * scale_ref[...]
    o_ref[0] = x + (m * _rms_scale(m)) * gpost_ref[...]


def _pool_mixer(x, g_pre, g_post, w_pool, pool_scale, *, ts=512):
    b, s, d = x.shape
    ng, gc, _ = w_pool.shape
    assert ng == len(POOL_WINDOWS) and ng * gc == d and s % ts == 0
    vec = pl.BlockSpec((1, d), lambda bi, si: (0, 0))
    return pl.pallas_call(
        functools.partial(_pool_kernel, ts=ts, gc=gc),
        out_shape=jax.ShapeDtypeStruct(x.shape, x.dtype),
        grid=(b, s // ts),
        in_specs=[
            pl.BlockSpec((1, ts, d), lambda bi, si: (bi, si, 0)),
            vec, vec,
            pl.BlockSpec((ng, gc, gc), lambda bi, si: (0, 0, 0)),
            vec,
        ],
        out_specs=pl.BlockSpec((1, ts, d), lambda bi, si: (bi, si, 0)),
        scratch_shapes=[pltpu.VMEM((POOL_HALO, d), F32)],
        compiler_params=pltpu.CompilerParams(
            dimension_semantics=("arbitrary", "arbitrary"),
            vmem_limit_bytes=V7X_VMEM_LIMIT_BYTES),
        name="pool_mixer",
    )(x, g_pre.reshape(1, d), g_post.reshape(1, d), w_pool, pool_scale.reshape(1, d))


def _ffn_kernel(x_ref, gpre_ref, gpost_ref, wg_ref, wu_ref, wd_ref, *rest, nj, cast_next):
    if cast_next:
        next_f32, o_ref, next_bf16, h_ref = rest[:3], rest[3], rest[4:7], rest[7]
        for src, dst in zip(next_f32, next_bf16):
            dst[...] = src[...].astype(BF16)
    else:
        o_ref, h_ref = rest
    j = pl.program_id(1)
    tm = x_ref.shape[0]
    whole = (slice(0, tm),)
    halves = (slice(0, tm // 2), slice(tm // 2, tm))

    def pre_norm(rows):
        x = x_ref[rows, :]
        h_ref[rows, :] = ((x * _rms_scale(x)) * gpre_ref[...]).astype(BF16)

    def swiglu(rows, first):
        h = h_ref[rows, :]
        gate = jnp.dot(h, wg_ref[...], preferred_element_type=F32)
        up = jnp.dot(h, wu_ref[...], preferred_element_type=F32)
        act = (gate * jax.nn.sigmoid(gate)) * up
        down = jnp.dot(act.astype(BF16), wd_ref[...], preferred_element_type=F32)
        o_ref[rows, :] = down if first else o_ref[rows, :] + down

    def finish(rows):
        m = o_ref[rows, :]
        o_ref[rows, :] = x_ref[rows, :] + (m * _rms_scale(m)) * gpost_ref[...]

    @pl.when(j == 0)
    def _():
        for rows in halves:
            pre_norm(rows)
        for rows in halves:
            swiglu(rows, True)

    @pl.when((j > 0) & (j < nj - 1))
    def _():
        for rows in whole:
            swiglu(rows, False)

    @pl.when(j == nj - 1)
    def _():
        for rows in halves:
            swiglu(rows, False)
        for rows in halves:
            finish(rows)


def _ffn(x2d, g_pre, g_post, w_gate, w_up, w_down, next_f32=None, *, tm=1024, tf=512):
    t, d = x2d.shape
    dff = w_gate.shape[1]
    assert t % tm == 0 and dff % tf == 0 and dff // tf >= 2
    ni, nj = t // tm, dff // tf
    vec = pl.BlockSpec((1, d), lambda i, j: (0, 0))
    in_specs = [
        pl.BlockSpec((tm, d), lambda i, j: (i, 0)),
        vec, vec,
        pl.BlockSpec((d, tf), lambda i, j: (0, j)),
        pl.BlockSpec((d, tf), lambda i, j: (0, j)),
        pl.BlockSpec((tf, d), lambda i, j: (j, 0)),
    ]
    args = [x2d, g_pre.reshape(1, d), g_post.reshape(1, d), w_gate, w_up, w_down]
    out_shape = [jax.ShapeDtypeStruct(x2d.shape, x2d.dtype)]
    out_specs = [pl.BlockSpec((tm, d), lambda i, j: (i, 0))]
    if next_f32 is not None:
        layer, *stacked = next_f32
        assert d % ni == 0
        dr = d // ni
        in_specs += [pl.BlockSpec((None, dr, tf), lambda i, j: (layer, i, j)),
                     pl.BlockSpec((None, dr, tf), lambda i, j: (layer, i, j)),
                     pl.BlockSpec((None, tf, dr), lambda i, j: (layer, j, i))]
        args += stacked
        out_shape += [jax.ShapeDtypeStruct(w.shape, BF16) for w in (w_gate, w_up, w_down)]
        out_specs += [pl.BlockSpec((dr, tf), lambda i, j: (i, j)),
                      pl.BlockSpec((dr, tf), lambda i, j: (i, j)),
                      pl.BlockSpec((tf, dr), lambda i, j: (j, i))]
    outs = pl.pallas_call(
        functools.partial(_ffn_kernel, nj=nj, cast_next=next_f32 is not None),
        out_shape=out_shape,
        grid=(ni, nj),
        in_specs=in_specs,
        out_specs=out_specs,
        scratch_shapes=[pltpu.VMEM((tm, d), BF16)],
        compiler_params=pltpu.CompilerParams(
            dimension_semantics=("arbitrary", "arbitrary"),
            vmem_limit_bytes=V7X_VMEM_LIMIT_BYTES),
        name="ffn",
    )(*args)
    return outs[0], tuple(outs[1:])


def _pair_layout():
    half = ROT_DIM // 2
    perm, member = [], []
    for slab in range(2):
        for lane in range(SUB_HEAD_DIM):
            if lane < ROT_DIM:
                sub, dim = lane // half, lane % half + slab * half
            else:
                sub, dim = slab, lane
            perm.append(sub * SUB_HEAD_DIM + dim)
            member.append(sub)
    return perm, member


def _qkv_kernel(x_ref, gk_ref, gq_ref, w_ref, cos_ref, sin_ref, k_ref, v_ref, q_ref,
                hk_ref, hq_ref, *, nk, nv, q_scale, tn, tile):
    j = pl.program_id(1)
    groups = [slice(r, r + tile) for r in range(0, x_ref.shape[0], tile)]

    def pre_norm(rows):
        x = x_ref[rows, :]
        xn = x * _rms_scale(x)
        hk_ref[rows, :] = (xn * gk_ref[...]).astype(BF16)
        hq_ref[rows, :] = (xn * gq_ref[...]).astype(BF16)

    def project(h_ref, out_ref, rope, scale, transposed):
        for ti, rows in enumerate(groups):
            for c0 in range(0, tn, 2 * SUB_HEAD_DIM):
                y = jnp.dot(h_ref[rows, :], w_ref[:, c0:c0 + 2 * SUB_HEAD_DIM], preferred_element_type=F32)
                a, b = y[:, :SUB_HEAD_DIM], y[:, SUB_HEAD_DIM:]
                if rope:
                    cos, sin = cos_ref[rows, :], sin_ref[rows, :]
                    a, b = a * cos - b * sin, b * cos + a * sin
                if scale is not None:
                    a, b = a * scale, b * scale
                if transposed:
                    out_ref[ti, c0:c0 + SUB_HEAD_DIM, :] = a.T.astype(BF16)
                    out_ref[ti, c0 + SUB_HEAD_DIM:c0 + 2 * SUB_HEAD_DIM, :] = b.T.astype(BF16)
                else:
                    out_ref[rows, c0:c0 + SUB_HEAD_DIM] = a.astype(BF16)
                    out_ref[rows, c0 + SUB_HEAD_DIM:c0 + 2 * SUB_HEAD_DIM] = b.astype(BF16)

    @pl.when(j == 0)
    def _():
        for rows in groups:
            pre_norm(rows)
        project(hk_ref, k_ref, True, None, False)

    @pl.when((j > 0) & (j < nk))
    def _():
        project(hk_ref, k_ref, True, None, False)

    @pl.when((j >= nk) & (j < nk + nv))
    def _():
        project(hk_ref, v_ref, False, None, True)

    @pl.when(j >= nk + nv)
    def _():
        project(hq_ref, q_ref, True, q_scale, True)


def _qkv_proj(x2d, g_kv, g_q, w_cat, rope_cos, rope_sin, *, d_k, d_v, d_q, q_scale, tile, tm=1024, tn=1024):
    t, d = x2d.shape
    s = rope_cos.shape[0]
    assert t % tm == 0 and s % tm == 0 and d_k % tn == 0 and d_v % tn == 0 and d_q % tn == 0 and tm % tile == 0
    nk, nv, nq = d_k // tn, d_v // tn, d_q // tn
    ns = s // tm
    vec = pl.BlockSpec((1, d), lambda i, j: (0, 0))
    tab = pl.BlockSpec((tm, SUB_HEAD_DIM), lambda i, j: (i % ns, 0))
    k_spec = pl.BlockSpec((tm, tn), lambda i, j: (i, jnp.clip(j, 0, nk - 1)))
    v_spec = pl.BlockSpec((tm // tile, tn, tile), lambda i, j: (i, jnp.clip(j - nk, 0, nv - 1), 0))
    q_spec = pl.BlockSpec((tm // tile, tn, tile), lambda i, j: (i, jnp.clip(j - nk - nv, 0, nq - 1), 0))
    tiled = lambda n: jax.ShapeDtypeStruct((t // tile, n, tile), BF16)
    return pl.pallas_call(
        functools.partial(_qkv_kernel, nk=nk, nv=nv, q_scale=q_scale, tn=tn, tile=tile),
        out_shape=(jax.ShapeDtypeStruct((t, d_k), BF16), tiled(d_v), tiled(d_q)),
        grid=(t // tm, nk + nv + nq),
        in_specs=[
            pl.BlockSpec((tm, d), lambda i, j: (i, 0)),
            vec, vec,
            pl.BlockSpec((d, tn), lambda i, j: (0, j)),
            tab, tab,
        ],
        out_specs=(k_spec, v_spec, q_spec),
        scratch_shapes=[pltpu.VMEM((tm, d), BF16), pltpu.VMEM((tm, d), BF16)],
        compiler_params=pltpu.CompilerParams(
            dimension_semantics=("arbitrary", "arbitrary"),
            vmem_limit_bytes=V7X_VMEM_LIMIT_BYTES),
        name="qkv_proj",
    )(x2d, g_kv.reshape(1, d), g_q.reshape(1, d), w_cat, rope_cos, rope_sin)


def _attn_kernel(q_ref, k_ref, v_ref, member_ref, lq1_ref, lk1_ref, lq2_ref, lk2_ref, g_ref, o_ref,
                 acc_ref, s_ref, *, tq, nq, lambda_init):
    member = member_ref[...]
    lam = (jnp.exp(jnp.sum(lq1_ref[...] * lk1_ref[...], axis=-1, keepdims=True))
           - jnp.exp(jnp.sum(lq2_ref[...] * lk2_ref[...], axis=-1, keepdims=True))
           + lambda_init)

    def q_tile(qi, diag_slot):
        base = qi % 2
        half = tq // 2
        q = q_ref[0, qi]
        qs = [jnp.where(member == i, q, jnp.zeros_like(q)) for i in range(2)]

        def causal(s):
            kpos = lax.broadcasted_iota(jnp.int32, s.shape, 0)
            qpos = lax.broadcasted_iota(jnp.int32, s.shape, 1)
            return jnp.where(kpos <= qpos, s, -jnp.inf)

        def diag_scores(slot):
            bmax = []
            k = k_ref[0, qi * tq:(qi + 1) * tq, :]
            for i in range(2):
                s_lo = jnp.dot(k[:half], qs[i], preferred_element_type=F32)
                s_lo = jnp.concatenate([causal(s_lo[:, :half]), s_lo[:, half:]], axis=1)
                s_hi = causal(jnp.dot(k[half:], qs[i][:, half:], preferred_element_type=F32))
                s_ref[slot, i, :half, :] = s_lo
                s_ref[slot, i, half:, half:] = s_hi
                b_lo = jnp.max(s_lo, axis=0, keepdims=True)
                b_hi = jnp.max(s_hi, axis=0, keepdims=True)
                bmax.append(jnp.concatenate([b_lo[:, :half], jnp.maximum(b_lo[:, half:], b_hi)], axis=1))
            return tuple(bmax)

        def diag_accumulate(slot, bmax):
            v = v_ref[0, qi]
            new = []
            for i in range(2):
                m_new = bmax[i]
                p_lo = jnp.exp2(s_ref[slot, i, :half, :] - m_new)
                p_hi = jnp.exp2(s_ref[slot, i, half:, half:] - m_new[:, half:])
                l_hi = jnp.sum(p_hi, axis=0, keepdims=True)
                l = jnp.sum(p_lo, axis=0, keepdims=True) + jnp.concatenate([jnp.zeros_like(l_hi), l_hi], axis=1)
                pv = jnp.dot(v[:, :half], p_lo.astype(BF16), preferred_element_type=F32)
                pv_hi = jnp.dot(v[:, half:], p_hi.astype(BF16), preferred_element_type=F32)
                acc_ref[base, i, :, :half] = pv[:, :half]
                acc_ref[base, i, :, half:] = pv[:, half:] + pv_hi
                new += [m_new, l]
            return tuple(new)

        def scores(ki, slot):
            bmax = []
            k = k_ref[0, pl.ds(ki * tq, tq), :]
            for i in range(2):
                s = jnp.dot(k, qs[i], preferred_element_type=F32)
                s_ref[slot, i] = s
                bmax.append(jnp.max(s, axis=0, keepdims=True))
            return tuple(bmax)

        def accumulate(ki, slot, state):
            carry, bmax = state[:4], state[4:]
            v = v_ref[0, ki]
            new = []
            for i in range(2):
                m, l = carry[2 * i], carry[2 * i + 1]
                m_new = jnp.maximum(m, bmax[i])
                alpha = jnp.exp2(m - m_new)
                p = jnp.exp2(s_ref[slot, i] - m_new)
                l = alpha * l + jnp.sum(p, axis=0, keepdims=True)
                pv = jnp.dot(v, p.astype(BF16), preferred_element_type=F32)
                acc_ref[base, i] = alpha * acc_ref[base, i] + pv
                new += [m_new, l]
            return tuple(new)

        bmax = diag_scores(diag_slot)
        slot = diag_slot
        if qi == 0:
            _, l1, _, l2 = diag_accumulate(slot, bmax)
        else:
            slot = 1 - diag_slot
            state = diag_accumulate(diag_slot, bmax) + scores(0, slot)
            first = 0
            if (qi - 1) % 2 == 1:
                state = accumulate(0, slot, state) + scores(1, diag_slot)
                first, slot = 1, diag_slot

            def pair(j, state):
                ki = first + 2 * j
                state = accumulate(ki, slot, state) + scores(ki + 1, 1 - slot)
                return accumulate(ki + 1, 1 - slot, state) + scores(ki + 2, slot)

            if (qi - 1) // 2 > 0:
                state = lax.fori_loop(0, (qi - 1) // 2, pair, state)
            _, l1, _, l2 = accumulate(qi - 1, slot, state)

        ot = acc_ref[base, 0] * (1.0 / l1) - lam * (acc_ref[base, 1] * (1.0 / l2))
        ot = ot * lax.rsqrt(jnp.mean(ot * ot, axis=0, keepdims=True) + EPS)
        o_ref[0, qi * tq:(qi + 1) * tq, :] = ((ot.T * g_ref[...]) * (1.0 - lambda_init)).astype(o_ref.dtype)
        return slot

    last_slot = 1
    for qi in range(nq):
        last_slot = q_tile(qi, 1 - last_slot)


def _diff_attn(q, k, v, lq1, lk1, lq2, lk2, subln_gain, lambda_init, *, tq=512):
    b, s, d = k.shape
    nh = d // V_HEAD_DIM
    nq = s // tq
    assert q.shape == v.shape == (b, nq, d, tq)
    lvec = pl.BlockSpec((1, SUB_HEAD_DIM), lambda bi, hi: (0, 0))
    hvec = pl.BlockSpec((1, V_HEAD_DIM), lambda bi, hi: (0, 0))
    member = jnp.asarray(_pair_layout()[1], jnp.int32).reshape(V_HEAD_DIM, 1)
    seq = pl.BlockSpec((1, s, V_HEAD_DIM), lambda bi, hi: (bi, 0, hi))
    tiles = pl.BlockSpec((1, nq, V_HEAD_DIM, tq), lambda bi, hi: (bi, 0, hi, 0))
    return pl.pallas_call(
        functools.partial(_attn_kernel, tq=tq, nq=nq, lambda_init=lambda_init),
        out_shape=jax.ShapeDtypeStruct((b, s, d), BF16),
        grid=(b, nh),
        in_specs=[tiles, seq, tiles, pl.BlockSpec((V_HEAD_DIM, 1), lambda bi, hi: (0, 0)),
                  lvec, lvec, lvec, lvec, hvec],
        out_specs=seq,
        scratch_shapes=[pltpu.VMEM((2, 2, V_HEAD_DIM, tq), F32), pltpu.VMEM((2, 2, tq, tq), F32)],
        compiler_params=pltpu.CompilerParams(
            dimension_semantics=("arbitrary", "arbitrary"),
            vmem_limit_bytes=V7X_VMEM_LIMIT_BYTES),
        name="diff_attn",
    )(q, k, v, member, lq1.reshape(1, -1), lk1.reshape(1, -1), lq2.reshape(1, -1), lk2.reshape(1, -1),
      subln_gain.reshape(1, -1))


def _oproj_kernel(a_ref, x_ref, w_ref, g_ref, o_ref):
    tm = x_ref.shape[0]
    for rows in (slice(0, tm // 2), slice(tm // 2, tm)):
        m = jnp.dot(a_ref[rows, :], w_ref[...], preferred_element_type=F32)
        o_ref[rows, :] = x_ref[rows, :] + (m * _rms_scale(m)) * g_ref[...]


def _out_proj(a2d, x2d, w_o, g_post, *, tm=512):
    t, d = x2d.shape
    assert t % tm == 0
    row = pl.BlockSpec((tm, d), lambda i: (i, 0))
    return pl.pallas_call(
        _oproj_kernel,
        out_shape=jax.ShapeDtypeStruct(x2d.shape, x2d.dtype),
        grid=(t // tm,),
        in_specs=[row, row,
                  pl.BlockSpec((d, d), lambda i: (0, 0)),
                  pl.BlockSpec((1, d), lambda i: (0, 0))],
        out_specs=row,
        compiler_params=pltpu.CompilerParams(
            dimension_semantics=("arbitrary",),
            vmem_limit_bytes=V7X_VMEM_LIMIT_BYTES),
        name="out_proj",
    )(a2d, x2d, w_o, g_post.reshape(1, d))


def _rope_tables(s):
    inv = ROPE_THETA ** (-jnp.arange(0, ROT_DIM, 2, dtype=F32) / ROT_DIM)
    ang = jnp.arange(s, dtype=F32)[:, None] * inv[None, :]
    cos, sin = jnp.cos(ang), jnp.sin(ang)
    rest = SUB_HEAD_DIM - ROT_DIM
    cos = jnp.concatenate([cos, cos, jnp.ones((s, rest), F32)], axis=-1)
    sin = jnp.concatenate([sin, sin, jnp.zeros((s, rest), F32)], axis=-1)
    return cos, sin


def _wprep_kernel(wkv_ref, wq_ref, o_ref, *, nk, nv, tn):
    j = pl.program_id(0)
    half = ROT_DIM // 2

    def pair_layout(w):
        lane = lax.broadcasted_iota(jnp.int32, (1, SUB_HEAD_DIM), 1)
        for c0 in range(0, tn, V_HEAD_DIM):
            a = w[:, c0:c0 + SUB_HEAD_DIM]
            b = w[:, c0 + SUB_HEAD_DIM:c0 + V_HEAD_DIM]
            s0 = jnp.where((lane >= half) & (lane < ROT_DIM), pltpu.roll(b, half, axis=1), a)
            s1 = jnp.where(lane < half, pltpu.roll(a, SUB_HEAD_DIM - half, axis=1), b)
            o_ref[:, c0:c0 + SUB_HEAD_DIM] = s0.astype(BF16)
            o_ref[:, c0 + SUB_HEAD_DIM:c0 + V_HEAD_DIM] = s1.astype(BF16)

    @pl.when(j < nk)
    def _():
        pair_layout(wkv_ref[...])

    @pl.when((j >= nk) & (j < nk + nv))
    def _():
        o_ref[...] = wkv_ref[...].astype(BF16)

    @pl.when(j >= nk + nv)
    def _():
        pair_layout(wq_ref[...])


def _prep_qkv_weights(w_kv, w_q, *, d_k, tn=512):
    d_in, d_kv = w_kv.shape
    d_q = w_q.shape[1]
    assert d_k % tn == 0 and d_kv % tn == 0 and d_q % tn == 0 and tn % V_HEAD_DIM == 0
    nk, nkv, nq = d_k // tn, d_kv // tn, d_q // tn
    return pl.pallas_call(
        functools.partial(_wprep_kernel, nk=nk, nv=nkv - nk, tn=tn),
        out_shape=jax.ShapeDtypeStruct((d_in, d_kv + d_q), BF16),
        grid=(nkv + nq,),
        in_specs=[pl.BlockSpec((d_in, tn), lambda j: (0, jnp.minimum(j, nkv - 1))),
                  pl.BlockSpec((d_in, tn), lambda j: (0, jnp.maximum(j - nkv, 0)))],
        out_specs=pl.BlockSpec((d_in, tn), lambda j: (0, j)),
        compiler_params=pltpu.CompilerParams(
            dimension_semantics=("arbitrary",),
            vmem_limit_bytes=V7X_VMEM_LIMIT_BYTES),
        name="qkv_weight_prep",
    )(w_kv, w_q)


def _lambda_init_for(layer_idx):
    return 0.8 - 0.6 * math.exp(-0.3 * layer_idx)


def kernel(x, norm_mix_pre, norm_mix_post, norm_ffn_pre, norm_ffn_post, w_pool, pool_scale, kv_norm, w_kv, w_q, lambda_q1, lambda_k1, lambda_q2, lambda_k2, subln_gain, w_o, w_ffn_gate, w_ffn_up, w_ffn_down):
    b, s, d = x.shape
    depth = norm_mix_pre.shape[0]
    n_a = w_pool.shape[0]
    rope_cos, rope_sin = _rope_tables(s)
    q_scale = SUB_HEAD_DIM ** -0.5 * LOG2E
    ffn_w = (w_ffn_gate[0].astype(BF16), w_ffn_up[0].astype(BF16), w_ffn_down[0].astype(BF16))
    k_sh = v_sh = None
    for l in range(depth):
        if l < n_a:
            x = _pool_mixer(x, norm_mix_pre[l], norm_mix_post[l], w_pool[l].astype(BF16), pool_scale[l])
        else:
            jb = l - n_a
            x2d = x.reshape(b * s, d)
            w_cat = _prep_qkv_weights(w_kv, w_q[jb], d_k=d)
            k_l, v_l, q = _qkv_proj(x2d, kv_norm, norm_mix_pre[l], w_cat, rope_cos, rope_sin,
                                    d_k=d, d_v=d, d_q=d, q_scale=q_scale, tile=ATTN_TILE)
            tiled = (b, s // ATTN_TILE, d, ATTN_TILE)
            if k_sh is None:
                k_sh, v_sh = k_l.reshape(b, s, d), v_l.reshape(tiled)
            a = _diff_attn(q.reshape(tiled), k_sh, v_sh, lambda_q1[jb], lambda_k1[jb], lambda_q2[jb],
                           lambda_k2[jb], subln_gain[jb], _lambda_init_for(l), tq=ATTN_TILE)
            x = _out_proj(a.reshape(b * s, d), x2d, w_o[jb].astype(BF16), norm_mix_post[l]).reshape(b, s, d)
        nxt = (l + 1, w_ffn_gate, w_ffn_up, w_ffn_down) if l + 1 < depth else None
        x, ffn_w = _ffn(x.reshape(b * s, d), norm_ffn_pre[l], norm_ffn_post[l], *ffn_w, next_f32=nxt)
        x = x.reshape(b, s, d)
    return x
```

```python
import functools
import math

import jax
import jax.numpy as jnp
from jax import lax
from jax.experimental import pallas as pl
from jax.experimental.pallas import tpu as pltpu

F32 = jnp.float32
BF16 = jnp.bfloat16

EPS = 1e-6
POOL_WINDOWS = (2, 4, 8, 16)
POOL_HALO = 16
SUB_HEAD_DIM = 128
V_HEAD_DIM = 2 * SUB_HEAD_DIM
ROT_DIM = SUB_HEAD_DIM // 4
ROPE_THETA = 500000.0
LOG2E = 1.4426950408889634
BF16_SUBLANES = 16
ATTN_TILE = 512

V7X_VMEM_LIMIT_BYTES = 58 * 1024 * 1024


def _rms_scale(x):
    return lax.rsqrt(jnp.mean(x * x, axis=-1, keepdims=True) + EPS)


def _pool_kernel(x_ref, gpre_ref, gpost_ref, w_ref, scale_ref, *rest, ts, gc, cast_ffn):
    if cast_ffn:
        ffn_f32, o_ref, ffn_bf16, halo_ref = rest[:3], rest[3], rest[4:7], rest[7]
        for src, dst in zip(ffn_f32, ffn_bf16):
            dst[...] = src[...].astype(BF16)
    else:
        o_ref, halo_ref = rest
    si = pl.program_id(1)

    @pl.when(si == 0)
    def _():
        halo_ref[...] = jnp.zeros_like(halo_ref)

    x = x_ref[0]
    h = (x * _rms_scale(x)) * gpre_ref[...]
    pos = si * ts + lax.broadcasted_iota(jnp.int32, (ts, 1), 0)
    outs = []
    for g, w in enumerate(POOL_WINDOWS):
        hg = h[:, g * gc:(g + 1) * gc]
        s = jnp.concatenate([halo_ref[:, g * gc:(g + 1) * gc], hg], axis=0)
        k = 1
        while k < w:
            s = s + pltpu.roll(s, k, axis=0)
            k *= 2
        cnt = jnp.minimum(pos + 1, w).astype(F32)
        mixed = s[POOL_HALO:, :] * (1.0 / cnt) - hg
        outs.append(jnp.dot(mixed.astype(BF16), w_ref[g], preferred_element_type=F32))
    halo_ref[...] = h[ts - POOL_HALO:, :]
    m = jnp.concatenate(outs, axis=-1) * scale_ref[...]
    o_ref[0] = x + (m * _rms_scale(m)) * gpost_ref[...]


def _pool_mixer(x, g_pre, g_post, w_pool, pool_scale, ffn_f32=None, *, ts=512):
    b, s, d = x.shape
    ng, gc, _ = w_pool.shape
    assert ng == len(POOL_WINDOWS) and ng * gc == d and s % ts == 0
    ns = s // ts
    vec = pl.BlockSpec((1, d), lambda bi, si: (0, 0))
    in_specs = [
        pl.BlockSpec((1, ts, d), lambda bi, si: (bi, si, 0)),
        vec, vec,
        pl.BlockSpec((ng, gc, gc), lambda bi, si: (0, 0, 0)),
        vec,
    ]
    args = [x, g_pre.reshape(1, d), g_post.reshape(1, d), w_pool, pool_scale.reshape(1, d)]
    out_shape = [jax.ShapeDtypeStruct(x.shape, x.dtype)]
    out_specs = [pl.BlockSpec((1, ts, d), lambda bi, si: (bi, si, 0))]
    if ffn_f32 is not None:
        layer, *stacked = ffn_f32
        steps = b * ns
        for w in stacked:
            rows = w.shape[1]
            assert rows % steps == 0 and (rows // steps) % BF16_SUBLANES == 0
            blk = (rows // steps, w.shape[2])
            in_specs.append(pl.BlockSpec((None,) + blk, lambda bi, si: (layer, bi * ns + si, 0)))
            out_shape.append(jax.ShapeDtypeStruct(w.shape[1:], BF16))
            out_specs.append(pl.BlockSpec(blk, lambda bi, si: (bi * ns + si, 0)))
        args += stacked
    outs = pl.pallas_call(
        functools.partial(_pool_kernel, ts=ts, gc=gc, cast_ffn=ffn_f32 is not None),
        out_shape=out_shape,
        grid=(b, ns),
        in_specs=in_specs,
        out_specs=out_specs,
        scratch_shapes=[pltpu.VMEM((POOL_HALO, d), F32)],
        compiler_params=pltpu.CompilerParams(
            dimension_semantics=("arbitrary", "arbitrary"),
            vmem_limit_bytes=V7X_VMEM_LIMIT_BYTES),
        name="pool_mixer",
    )(*args)
    return outs[0], tuple(outs[1:])


def _ffn_kernel(x_ref, gpre_ref, gpost_ref, wg_ref, wu_ref, wd_ref, *rest, nj, cast_next):
    if cast_next:
        next_f32, o_ref, next_bf16, h_ref = rest[:3], rest[3], rest[4:7], rest[7]
        for src, dst in zip(next_f32, next_bf16):
            dst[...] = src[...].astype(BF16)
    else:
        o_ref, h_ref = rest
    j = pl.program_id(1)
    tm = x_ref.shape[0]
    whole = (slice(0, tm),)
    halves = (slice(0, tm // 2), slice(tm // 2, tm))

    def pre_norm(rows):
        x = x_ref[rows, :]
        h_ref[rows, :] = ((x * _rms_scale(x)) * gpre_ref[...]).astype(BF16)

    def swiglu(rows, first):
        h = h_ref[rows, :]
        gate = jnp.dot(h, wg_ref[...], preferred_element_type=F32)
        up = jnp.dot(h, wu_ref[...], preferred_element_type=F32)
        act = (gate * jax.nn.sigmoid(gate)) * up
        down = jnp.dot(act.astype(BF16), wd_ref[...], preferred_element_type=F32)
        o_ref[rows, :] = down if first else o_ref[rows, :] + down

    def finish(rows):
        m = o_ref[rows, :]
        o_ref[rows, :] = x_ref[rows, :] + (m * _rms_scale(m)) * gpost_ref[...]

    @pl.when(j == 0)
    def _():
        for rows in halves:
            pre_norm(rows)
        for rows in halves:
            swiglu(rows, True)

    @pl.when((j > 0) & (j < nj - 1))
    def _():
        for rows in whole:
            swiglu(rows, False)

    @pl.when(j == nj - 1)
    def _():
        for rows in halves:
            swiglu(rows, False)
        for rows in halves:
            finish(rows)


def _ffn(x2d, g_pre, g_post, w_gate, w_up, w_down, next_f32=None, *, tm=1024, tf=512):
    t, d = x2d.shape
    dff = w_gate.shape[1]
    assert t % tm == 0 and dff % tf == 0 and dff // tf >= 2
    ni, nj = t // tm, dff // tf
    vec = pl.BlockSpec((1, d), lambda i, j: (0, 0))
    in_specs = [
        pl.BlockSpec((tm, d), lambda i, j: (i, 0)),
        vec, vec,
        pl.BlockSpec((d, tf), lambda i, j: (0, j)),
        pl.BlockSpec((d, tf), lambda i, j: (0, j)),
        pl.BlockSpec((tf, d), lambda i, j: (j, 0)),
    ]
    args = [x2d, g_pre.reshape(1, d), g_post.reshape(1, d), w_gate, w_up, w_down]
    out_shape = [jax.ShapeDtypeStruct(x2d.shape, x2d.dtype)]
    out_specs = [pl.BlockSpec((tm, d), lambda i, j: (i, 0))]
    if next_f32 is not None:
        layer, *stacked = next_f32
        assert d % ni == 0
        dr = d // ni
        in_specs += [pl.BlockSpec((None, dr, tf), lambda i, j: (layer, i, j)),
                     pl.BlockSpec((None, dr, tf), lambda i, j: (layer, i, j)),
                     pl.BlockSpec((None, tf, dr), lambda i, j: (layer, j, i))]
        args += stacked
        out_shape += [jax.ShapeDtypeStruct(w.shape, BF16) for w in (w_gate, w_up, w_down)]
        out_specs += [pl.BlockSpec((dr, tf), lambda i, j: (i, j)),
                      pl.BlockSpec((dr, tf), lambda i, j: (i, j)),
                      pl.BlockSpec((tf, dr), lambda i, j: (j, i))]
    outs = pl.pallas_call(
        functools.partial(_ffn_kernel, nj=nj, cast_next=next_f32 is not None),
        out_shape=out_shape,
        grid=(ni, nj),
        in_specs=in_specs,
        out_specs=out_specs,
        scratch_shapes=[pltpu.VMEM((tm, d), BF16)],
        compiler_params=pltpu.CompilerParams(
            dimension_semantics=("arbitrary", "arbitrary"),
            vmem_limit_bytes=V7X_VMEM_LIMIT_BYTES),
        name="ffn",
    )(*args)
    return outs[0], tuple(outs[1:])


def _pair_layout():
    half = ROT_DIM // 2
    perm, member = [], []
    for slab in range(2):
        for lane in range(SUB_HEAD_DIM):
            if lane < ROT_DIM:
                sub, dim = lane // half, lane % half + slab * half
            else:
                sub, dim = slab, lane
            perm.append(sub * SUB_HEAD_DIM + dim)
            member.append(sub)
    return perm, member


def _qkv_kernel(x_ref, gk_ref, gq_ref, w_ref, cos_ref, sin_ref, k_ref, v_ref, q_ref,
                hk_ref, hq_ref, *, nk, nv, q_scale, tn, tile):
    j = pl.program_id(1)
    groups = [slice(r, r + tile) for r in range(0, x_ref.shape[0], tile)]

    def pre_norm(rows):
        x = x_ref[rows, :]
        xn = x * _rms_scale(x)
        hk_ref[rows, :] = (xn * gk_ref[...]).astype(BF16)
        hq_ref[rows, :] = (xn * gq_ref[...]).astype(BF16)

    def project(h_ref, out_ref, rope, scale, transposed):
        for ti, rows in enumerate(groups):
            for c0 in range(0, tn, 2 * SUB_HEAD_DIM):
                y = jnp.dot(h_ref[rows, :], w_ref[:, c0:c0 + 2 * SUB_HEAD_DIM], preferred_element_type=F32)
                a, b = y[:, :SUB_HEAD_DIM], y[:, SUB_HEAD_DIM:]
                if rope:
                    cos, sin = cos_ref[rows, :], sin_ref[rows, :]
                    a, b = a * cos - b * sin, b * cos + a * sin
                if scale is not None:
                    a, b = a * scale, b * scale
                if transposed:
                    out_ref[ti, c0:c0 + SUB_HEAD_DIM, :] = a.T.astype(BF16)
                    out_ref[ti, c0 + SUB_HEAD_DIM:c0 + 2 * SUB_HEAD_DIM, :] = b.T.astype(BF16)
                else:
                    out_ref[rows, c0:c0 + SUB_HEAD_DIM] = a.astype(BF16)
                    out_ref[rows, c0 + SUB_HEAD_DIM:c0 + 2 * SUB_HEAD_DIM] = b.astype(BF16)

    @pl.when(j == 0)
    def _():
        for rows in groups:
            pre_norm(rows)
        project(hk_ref, k_ref, True, None, False)

    @pl.when((j > 0) & (j < nk))
    def _():
        project(hk_ref, k_ref, True, None, False)

    @pl.when((j >= nk) & (j < nk + nv))
    def _():
        project(hk_ref, v_ref, False, None, True)

    @pl.when(j >= nk + nv)
    def _():
        project(hq_ref, q_ref, True, q_scale, True)


def _qkv_proj(x2d, g_kv, g_q, w_cat, rope_cos, rope_sin, *, d_k, d_v, d_q, q_scale, tile, tm=1024, tn=1024):
    t, d = x2d.shape
    s = rope_cos.shape[0]
    assert t % tm == 0 and s % tm == 0 and d_k % tn == 0 and d_v % tn == 0 and d_q % tn == 0 and tm % tile == 0
    nk, nv, nq = d_k // tn, d_v // tn, d_q // tn
    ns = s // tm
    vec = pl.BlockSpec((1, d), lambda i, j: (0, 0))
    tab = pl.BlockSpec((tm, SUB_HEAD_DIM), lambda i, j: (i % ns, 0))
    k_spec = pl.BlockSpec((tm, tn), lambda i, j: (i, jnp.clip(j, 0, nk - 1)))
    v_spec = pl.BlockSpec((tm // tile, tn, tile), lambda i, j: (i, jnp.clip(j - nk, 0, nv - 1), 0))
    q_spec = pl.BlockSpec((tm // tile, tn, tile), lambda i, j: (i, jnp.clip(j - nk - nv, 0, nq - 1), 0))
    tiled = lambda n: jax.ShapeDtypeStruct((t // tile, n, tile), BF16)
    return pl.pallas_call(
        functools.partial(_qkv_kernel, nk=nk, nv=nv, q_scale=q_scale, tn=tn, tile=tile),
        out_shape=(jax.ShapeDtypeStruct((t, d_k), BF16), tiled(d_v), tiled(d_q)),
        grid=(t // tm, nk + nv + nq),
        in_specs=[
            pl.BlockSpec((tm, d), lambda i, j: (i, 0)),
            vec, vec,
            pl.BlockSpec((d, tn), lambda i, j: (0, j)),
            tab, tab,
        ],
        out_specs=(k_spec, v_spec, q_spec),
        scratch_shapes=[pltpu.VMEM((tm, d), BF16), pltpu.VMEM((tm, d), BF16)],
        compiler_params=pltpu.CompilerParams(
            dimension_semantics=("arbitrary", "arbitrary"),
            vmem_limit_bytes=V7X_VMEM_LIMIT_BYTES),
        name="qkv_proj",
    )(x2d, g_kv.reshape(1, d), g_q.reshape(1, d), w_cat, rope_cos, rope_sin)


def _attn_kernel(q_ref, k_ref, v_ref, member_ref, lq1_ref, lk1_ref, lq2_ref, lk2_ref, g_ref, o_ref,
                 acc_ref, s_ref, *, tq, nq, lambda_init):
    member = member_ref[...]
    lam = (jnp.exp(jnp.sum(lq1_ref[...] * lk1_ref[...], axis=-1, keepdims=True))
           - jnp.exp(jnp.sum(lq2_ref[...] * lk2_ref[...], axis=-1, keepdims=True))
           + lambda_init)

    def q_tile(qi, diag_slot):
        base = qi % 2
        half = tq // 2
        q = q_ref[0, qi]
        qs = [jnp.where(member == i, q, jnp.zeros_like(q)) for i in range(2)]

        def causal(s):
            kpos = lax.broadcasted_iota(jnp.int32, s.shape, 0)
            qpos = lax.broadcasted_iota(jnp.int32, s.shape, 1)
            return jnp.where(kpos <= qpos, s, -jnp.inf)

        def diag_scores(slot):
            bmax = []
            k = k_ref[0, qi * tq:(qi + 1) * tq, :]
            for i in range(2):
                s_lo = jnp.dot(k[:half], qs[i], preferred_element_type=F32)
                s_lo = jnp.concatenate([causal(s_lo[:, :half]), s_lo[:, half:]], axis=1)
                s_hi = causal(jnp.dot(k[half:], qs[i][:, half:], preferred_element_type=F32))
                s_ref[slot, i, :half, :] = s_lo
                s_ref[slot, i, half:, half:] = s_hi
                b_lo = jnp.max(s_lo, axis=0, keepdims=True)
                b_hi = jnp.max(s_hi, axis=0, keepdims=True)
                bmax.append(jnp.concatenate([b_lo[:, :half], jnp.maximum(b_lo[:, half:], b_hi)], axis=1))
            return tuple(bmax)

        def diag_accumulate(slot, bmax):
            v = v_ref[0, qi]
            new = []
            for i in range(2):
                m_new = bmax[i]
                p_lo = jnp.exp2(s_ref[slot, i, :half, :] - m_new)
                p_hi = jnp.exp2(s_ref[slot, i, half:, half:] - m_new[:, half:])
                l_hi = jnp.sum(p_hi, axis=0, keepdims=True)
                l = jnp.sum(p_lo, axis=0, keepdims=True) + jnp.concatenate([jnp.zeros_like(l_hi), l_hi], axis=1)
                pv = jnp.dot(v[:, :half], p_lo.astype(BF16), preferred_element_type=F32)
                pv_hi = jnp.dot(v[:, half:], p_hi.astype(BF16), preferred_element_type=F32)
                acc_ref[base, i, :, :half] = pv[:, :half]
                acc_ref[base, i, :, half:] = pv[:, half:] + pv_hi
                new += [m_new, l]
            return tuple(new)

        def scores(ki, slot):
            bmax = []
            k = k_ref[0, pl.ds(ki * tq, tq), :]
            for i in range(2):
                s = jnp.dot(k, qs[i], preferred_element_type=F32)
                s_ref[slot, i] = s
                bmax.append(jnp.max(s, axis=0, keepdims=True))
            return tuple(bmax)

        def accumulate(ki, slot, state):
            carry, bmax = state[:4], state[4:]
            v = v_ref[0, ki]
            new = []
            for i in range(2):
                m, l = carry[2 * i], carry[2 * i + 1]
                m_new = jnp.maximum(m, bmax[i])
                alpha = jnp.exp2(m - m_new)
                p = jnp.exp2(s_ref[slot, i] - m_new)
                l = alpha * l + jnp.sum(p, axis=0, keepdims=True)
                pv = jnp.dot(v, p.astype(BF16), preferred_element_type=F32)
                acc_ref[base, i] = alpha * acc_ref[base, i] + pv
                new += [m_new, l]
            return tuple(new)

        bmax = diag_scores(diag_slot)
        slot = diag_slot
        if qi == 0:
            _, l1, _, l2 = diag_accumulate(slot, bmax)
        else:
            slot = 1 - diag_slot
            state = diag_accumulate(diag_slot, bmax) + scores(0, slot)
            first = 0
            if (qi - 1) % 2 == 1:
                state = accumulate(0, slot, state) + scores(1, diag_slot)
                first, slot = 1, diag_slot

            def pair(j, state):
                ki = first + 2 * j
                state = accumulate(ki, slot, state) + scores(ki + 1, 1 - slot)
                return accumulate(ki + 1, 1 - slot, state) + scores(ki + 2, slot)

            if (qi - 1) // 2 > 0:
                state = lax.fori_loop(0, (qi - 1) // 2, pair, state)
            _, l1, _, l2 = accumulate(qi - 1, slot, state)

        ot = acc_ref[base, 0] * (1.0 / l1) - lam * (acc_ref[base, 1] * (1.0 / l2))
        ot = ot * lax.rsqrt(jnp.mean(ot * ot, axis=0, keepdims=True) + EPS)
        o_ref[0, qi * tq:(qi + 1) * tq, :] = ((ot.T * g_ref[...]) * (1.0 - lambda_init)).astype(o_ref.dtype)
        return slot

    last_slot = 1
    for qi in range(nq):
        last_slot = q_tile(qi, 1 - last_slot)


def _diff_attn(q, k, v, lq1, lk1, lq2, lk2, subln_gain, lambda_init, *, tq=512):
    b, s, d = k.shape
    nh = d // V_HEAD_DIM
    nq = s // tq
    assert q.shape == v.shape == (b, nq, d, tq)
    lvec = pl.BlockSpec((1, SUB_HEAD_DIM), lambda bi, hi: (0, 0))
    hvec = pl.BlockSpec((1, V_HEAD_DIM), lambda bi, hi: (0, 0))
    member = jnp.asarray(_pair_layout()[1], jnp.int32).reshape(V_HEAD_DIM, 1)
    seq = pl.BlockSpec((1, s, V_HEAD_DIM), lambda bi, hi: (bi, 0, hi))
    tiles = pl.BlockSpec((1, nq, V_HEAD_DIM, tq), lambda bi, hi: (bi, 0, hi, 0))
    return pl.pallas_call(
        functools.partial(_attn_kernel, tq=tq, nq=nq, lambda_init=lambda_init),
        out_shape=jax.ShapeDtypeStruct((b, s, d), BF16),
        grid=(b, nh),
        in_specs=[tiles, seq, tiles, pl.BlockSpec((V_HEAD_DIM, 1), lambda bi, hi: (0, 0)),
                  lvec, lvec, lvec, lvec, hvec],
        out_specs=seq,
        scratch_shapes=[pltpu.VMEM((2, 2, V_HEAD_DIM, tq), F32), pltpu.VMEM((2, 2, tq, tq), F32)],
        compiler_params=pltpu.CompilerParams(
            dimension_semantics=("arbitrary", "arbitrary"),
            vmem_limit_bytes=V7X_VMEM_LIMIT_BYTES),
        name="diff_attn",
    )(q, k, v, member, lq1.reshape(1, -1), lk1.reshape(1, -1), lq2.reshape(1, -1), lk2.reshape(1, -1),
      subln_gain.reshape(1, -1))


def _oproj_kernel(a_ref, x_ref, w_ref, g_ref, o_ref):
    tm = x_ref.shape[0]
    for rows in (slice(0, tm // 2), slice(tm // 2, tm)):
        m = jnp.dot(a_ref[rows, :], w_ref[...], preferred_element_type=F32)
        o_ref[rows, :] = x_ref[rows, :] + (m * _rms_scale(m)) * g_ref[...]


def _out_proj(a2d, x2d, w_o, g_post, *, tm=512):
    t, d = x2d.shape
    assert t % tm == 0
    row = pl.BlockSpec((tm, d), lambda i: (i, 0))
    return pl.pallas_call(
        _oproj_kernel,
        out_shape=jax.ShapeDtypeStruct(x2d.shape, x2d.dtype),
        grid=(t // tm,),
        in_specs=[row, row,
                  pl.BlockSpec((d, d), lambda i: (0, 0)),
                  pl.BlockSpec((1, d), lambda i: (0, 0))],
        out_specs=row,
        compiler_params=pltpu.CompilerParams(
            dimension_semantics=("arbitrary",),
            vmem_limit_bytes=V7X_VMEM_LIMIT_BYTES),
        name="out_proj",
    )(a2d, x2d, w_o, g_post.reshape(1, d))


def _rope_tables(s):
    inv = ROPE_THETA ** (-jnp.arange(0, ROT_DIM, 2, dtype=F32) / ROT_DIM)
    ang = jnp.arange(s, dtype=F32)[:, None] * inv[None, :]
    cos, sin = jnp.cos(ang), jnp.sin(ang)
    rest = SUB_HEAD_DIM - ROT_DIM
    cos = jnp.concatenate([cos, cos, jnp.ones((s, rest), F32)], axis=-1)
    sin = jnp.concatenate([sin, sin, jnp.zeros((s, rest), F32)], axis=-1)
    return cos, sin


def _wprep_kernel(wkv_ref, wq_ref, o_ref, *, nk, nv, tn):
    j = pl.program_id(0)
    half = ROT_DIM // 2

    def pair_layout(w):
        lane = lax.broadcasted_iota(jnp.int32, (1, SUB_HEAD_DIM), 1)
        for c0 in range(0, tn, V_HEAD_DIM):
            a = w[:, c0:c0 + SUB_HEAD_DIM]
            b = w[:, c0 + SUB_HEAD_DIM:c0 + V_HEAD_DIM]
            s0 = jnp.where((lane >= half) & (lane < ROT_DIM), pltpu.roll(b, half, axis=1), a)
            s1 = jnp.where(lane < half, pltpu.roll(a, SUB_HEAD_DIM - half, axis=1), b)
            o_ref[:, c0:c0 + SUB_HEAD_DIM] = s0.astype(BF16)
            o_ref[:, c0 + SUB_HEAD_DIM:c0 + V_HEAD_DIM] = s1.astype(BF16)

    @pl.when(j < nk)
    def _():
        pair_layout(wkv_ref[...])

    @pl.when((j >= nk) & (j < nk + nv))
    def _():
        o_ref[...] = wkv_ref[...].astype(BF16)

    @pl.when(j >= nk + nv)
    def _():
        pair_layout(wq_ref[...])


def _prep_qkv_weights(w_kv, w_q, *, d_k, tn=512):
    d_in, d_kv = w_kv.shape
    d_q = w_q.shape[1]
    assert d_k % tn == 0 and d_kv % tn == 0 and d_q % tn == 0 and tn % V_HEAD_DIM == 0
    nk, nkv, nq = d_k // tn, d_kv // tn, d_q // tn
    return pl.pallas_call(
        functools.partial(_wprep_kernel, nk=nk, nv=nkv - nk, tn=tn),
        out_shape=jax.ShapeDtypeStruct((d_in, d_kv + d_q), BF16),
        grid=(nkv + nq,),
        in_specs=[pl.BlockSpec((d_in, tn), lambda j: (0, jnp.minimum(j, nkv - 1))),
                  pl.BlockSpec((d_in, tn), lambda j: (0, jnp.maximum(j - nkv, 0)))],
        out_specs=pl.BlockSpec((d_in, tn), lambda j: (0, j)),
        compiler_params=pltpu.CompilerParams(
            dimension_semantics=("arbitrary",),
            vmem_limit_bytes=V7X_VMEM_LIMIT_BYTES),
        name="qkv_weight_prep",
    )(w_kv, w_q)


def _lambda_init_for(layer_idx):
    return 0.8 - 0.6 * math.exp(-0.3 * layer_idx)


def kernel(x, norm_mix_pre, norm_mix_post, norm_ffn_pre, norm_ffn_post, w_pool, pool_scale, kv_norm, w_kv, w_q, lambda_q1, lambda_k1, lambda_q2, lambda_k2, subln_gain, w_o, w_ffn_gate, w_ffn_up, w_ffn_down):
    b, s, d = x.shape
    depth = norm_mix_pre.shape[0]
    n_a = w_pool.shape[0]
    rope_cos, rope_sin = _rope_tables(s)
    q_scale = SUB_HEAD_DIM ** -0.5 * LOG2E
    ffn_f32 = (w_ffn_gate, w_ffn_up, w_ffn_down)
    ffn_w = None
    k_sh = v_sh = None
    for l in range(depth):
        if l < n_a:
            x, ffn_w = _pool_mixer(x, norm_mix_pre[l], norm_mix_post[l], w_pool[l].astype(BF16), pool_scale[l],
                                   ffn_f32=(l,) + ffn_f32)
        else:
            jb = l - n_a
            x2d = x.reshape(b * s, d)
            w_cat = _prep_qkv_weights(w_kv, w_q[jb], d_k=d)
            k_l, v_l, q = _qkv_proj(x2d, kv_norm, norm_mix_pre[l], w_cat, rope_cos, rope_sin,
                                    d_k=d, d_v=d, d_q=d, q_scale=q_scale, tile=ATTN_TILE)
            tiled = (b, s // ATTN_TILE, d, ATTN_TILE)
            if k_sh is None:
                k_sh, v_sh = k_l.reshape(b, s, d), v_l.reshape(tiled)
            a = _diff_attn(q.reshape(tiled), k_sh, v_sh, lambda_q1[jb], lambda_k1[jb], lambda_q2[jb],
                           lambda_k2[jb], subln_gain[jb], _lambda_init_for(l), tq=ATTN_TILE)
            x = _out_proj(a.reshape(b * s, d), x2d, w_o[jb].astype(BF16), norm_mix_post[l]).reshape(b, s, d)
        if ffn_w is None:
            ffn_w = tuple(w[l].astype(BF16) for w in ffn_f32)
        nxt = (l + 1,) + ffn_f32 if n_a <= l + 1 < depth else None
        x, ffn_w = _ffn(x.reshape(b * s, d), norm_ffn_pre[l], norm_ffn_post[l], *ffn_w, next_f32=nxt)
        ffn_w = ffn_w or None
        x = x.reshape(b, s, d)
    return x
```

```python
import functools
import math

import jax
import jax.numpy as jnp
from jax import lax
from jax.experimental import pallas as pl
from jax.experimental.pallas import tpu as pltpu

F32 = jnp.float32
BF16 = jnp.bfloat16

EPS = 1e-6
POOL_WINDOWS = (2, 4, 8, 16)
POOL_HALO = 16
SUB_HEAD_DIM = 128
V_HEAD_DIM = 2 * SUB_HEAD_DIM
ROT_DIM = SUB_HEAD_DIM // 4
ROPE_THETA = 500000.0
LOG2E = 1.4426950408889634
BF16_SUBLANES = 16
ATTN_TILE = 512

V7X_VMEM_LIMIT_BYTES = 58 * 1024 * 1024


def _rms_scale(x):
    return lax.rsqrt(jnp.mean(x * x, axis=-1, keepdims=True) + EPS)


def _pool_kernel(x_ref, gpre_ref, gpost_ref, w_ref, scale_ref, *rest, ts, gc, cast_ffn):
    if cast_ffn:
        ffn_f32, o_ref, ffn_bf16, halo_ref = rest[:3], rest[3], rest[4:7], rest[7]
        for src, dst in zip(ffn_f32, ffn_bf16):
            dst[...] = src[...].astype(BF16)
    else:
        o_ref, halo_ref = rest
    si = pl.program_id(1)

    @pl.when(si == 0)
    def _():
        halo_ref[...] = jnp.zeros_like(halo_ref)

    x = x_ref[0]
    h = (x * _rms_scale(x)) * gpre_ref[...]
    pos = si * ts + lax.broadcasted_iota(jnp.int32, (ts, 1), 0)
    outs = []
    for g, w in enumerate(POOL_WINDOWS):
        hg = h[:, g * gc:(g + 1) * gc]
        s = jnp.concatenate([halo_ref[:, g * gc:(g + 1) * gc], hg], axis=0)
        k = 1
        while k < w:
            s = s + pltpu.roll(s, k, axis=0)
            k *= 2
        cnt = jnp.minimum(pos + 1, w).astype(F32)
        mixed = s[POOL_HALO:, :] * (1.0 / cnt) - hg
        outs.append(jnp.dot(mixed.astype(BF16), w_ref[g], preferred_element_type=F32))
    halo_ref[...] = h[ts - POOL_HALO:, :]
    m = jnp.concatenate(outs, axis=-1) * scale_ref[...]
    o_ref[0] = x + (m * _rms_scale(m)) * gpost_ref[...]


def _pool_mixer(x, g_pre, g_post, w_pool, pool_scale, ffn_f32=None, *, ts=512):
    b, s, d = x.shape
    ng, gc, _ = w_pool.shape
    assert ng == len(POOL_WINDOWS) and ng * gc == d and s % ts == 0
    ns = s // ts
    vec = pl.BlockSpec((1, d), lambda bi, si: (0, 0))
    in_specs = [
        pl.BlockSpec((1, ts, d), lambda bi, si: (bi, si, 0)),
        vec, vec,
        pl.BlockSpec((ng, gc, gc), lambda bi, si: (0, 0, 0)),
        vec,
    ]
    args = [x, g_pre.reshape(1, d), g_post.reshape(1, d), w_pool, pool_scale.reshape(1, d)]
    out_shape = [jax.ShapeDtypeStruct(x.shape, x.dtype)]
    out_specs = [pl.BlockSpec((1, ts, d), lambda bi, si: (bi, si, 0))]
    if ffn_f32 is not None:
        layer, *stacked = ffn_f32
        steps = b * ns
        for w in stacked:
            rows = w.shape[1]
            assert rows % steps == 0 and (rows // steps) % BF16_SUBLANES == 0
            blk = (rows // steps, w.shape[2])
            in_specs.append(pl.BlockSpec((None,) + blk, lambda bi, si: (layer, bi * ns + si, 0)))
            out_shape.append(jax.ShapeDtypeStruct(w.shape[1:], BF16))
            out_specs.append(pl.BlockSpec(blk, lambda bi, si: (bi * ns + si, 0)))
        args += stacked
    outs = pl.pallas_call(
        functools.partial(_pool_kernel, ts=ts, gc=gc, cast_ffn=ffn_f32 is not None),
        out_shape=out_shape,
        grid=(b, ns),
        in_specs=in_specs,
        out_specs=out_specs,
        scratch_shapes=[pltpu.VMEM((POOL_HALO, d), F32)],
        compiler_params=pltpu.CompilerParams(
            dimension_semantics=("arbitrary", "arbitrary"),
            vmem_limit_bytes=V7X_VMEM_LIMIT_BYTES),
        name="pool_mixer",
    )(*args)
    return outs[0], tuple(outs[1:])


def _ffn_kernel(x_ref, gpre_ref, gpost_ref, wg_ref, wu_ref, wd_ref, *rest, nj, cast_next):
    if cast_next:
        next_f32, o_ref, next_bf16, h_ref = rest[:3], rest[3], rest[4:7], rest[7]
        for src, dst in zip(next_f32, next_bf16):
            dst[...] = src[...].astype(BF16)
    else:
        o_ref, h_ref = rest
    j = pl.program_id(1)
    tm = x_ref.shape[0]
    whole = (slice(0, tm),)
    halves = (slice(0, tm // 2), slice(tm // 2, tm))

    def pre_norm(rows):
        x = x_ref[rows, :]
        h_ref[rows, :] = ((x * _rms_scale(x)) * gpre_ref[...]).astype(BF16)

    def swiglu(rows, first):
        h = h_ref[rows, :]
        gate = jnp.dot(h, wg_ref[...], preferred_element_type=F32)
        up = jnp.dot(h, wu_ref[...], preferred_element_type=F32)
        act = (gate * jax.nn.sigmoid(gate)) * up
        down = jnp.dot(act.astype(BF16), wd_ref[...], preferred_element_type=F32)
        o_ref[rows, :] = down if first else o_ref[rows, :] + down

    def finish(rows):
        m = o_ref[rows, :]
        o_ref[rows, :] = x_ref[rows, :] + (m * _rms_scale(m)) * gpost_ref[...]

    @pl.when(j == 0)
    def _():
        for rows in halves:
            pre_norm(rows)
        for rows in halves:
            swiglu(rows, True)

    @pl.when((j > 0) & (j < nj - 1))
    def _():
        for rows in whole:
            swiglu(rows, False)

    @pl.when(j == nj - 1)
    def _():
        for rows in halves:
            swiglu(rows, False)
        for rows in halves:
            finish(rows)


def _ffn(x2d, g_pre, g_post, w_gate, w_up, w_down, next_f32=None, *, tm=1024, tf=512):
    t, d = x2d.shape
    dff = w_gate.shape[1]
    assert t % tm == 0 and dff % tf == 0 and dff // tf >= 2
    ni, nj = t // tm, dff // tf
    vec = pl.BlockSpec((1, d), lambda i, j: (0, 0))
    in_specs = [
        pl.BlockSpec((tm, d), lambda i, j: (i, 0)),
        vec, vec,
        pl.BlockSpec((d, tf), lambda i, j: (0, j)),
        pl.BlockSpec((d, tf), lambda i, j: (0, j)),
        pl.BlockSpec((tf, d), lambda i, j: (j, 0)),
    ]
    args = [x2d, g_pre.reshape(1, d), g_post.reshape(1, d), w_gate, w_up, w_down]
    out_shape = [jax.ShapeDtypeStruct(x2d.shape, x2d.dtype)]
    out_specs = [pl.BlockSpec((tm, d), lambda i, j: (i, 0))]
    if next_f32 is not None:
        layer, *stacked = next_f32
        assert d % ni == 0
        dr = d // ni
        in_specs += [pl.BlockSpec((None, dr, tf), lambda i, j: (layer, i, j)),
                     pl.BlockSpec((None, dr, tf), lambda i, j: (layer, i, j)),
                     pl.BlockSpec((None, tf, dr), lambda i, j: (layer, j, i))]
        args += stacked
        out_shape += [jax.ShapeDtypeStruct(w.shape, BF16) for w in (w_gate, w_up, w_down)]
        out_specs += [pl.BlockSpec((dr, tf), lambda i, j: (i, j)),
                      pl.BlockSpec((dr, tf), lambda i, j: (i, j)),
                      pl.BlockSpec((tf, dr), lambda i, j: (j, i))]
    outs = pl.pallas_call(
        functools.partial(_ffn_kernel, nj=nj, cast_next=next_f32 is not None),
        out_shape=out_shape,
        grid=(ni, nj),
        in_specs=in_specs,
        out_specs=out_specs,
        scratch_shapes=[pltpu.VMEM((tm, d), BF16)],
        compiler_params=pltpu.CompilerParams(
            dimension_semantics=("arbitrary", "arbitrary"),
            vmem_limit_bytes=V7X_VMEM_LIMIT_BYTES),
        name="ffn",
    )(*args)
    return outs[0], tuple(outs[1:])


def _pair_layout():
    half = ROT_DIM // 2
    perm, member = [], []
    for slab in range(2):
        for lane in range(SUB_HEAD_DIM):
            if lane < ROT_DIM:
                sub, dim = lane // half, lane % half + slab * half
            else:
                sub, dim = slab, lane
            perm.append(sub * SUB_HEAD_DIM + dim)
            member.append(sub)
    return perm, member


def _qkv_kernel(x_ref, gk_ref, gq_ref, w_ref, cos_ref, sin_ref, k_ref, v_ref, q_ref,
                hk_ref, hq_ref, *, nk, nv, q_scale, tn, tile):
    j = pl.program_id(1)
    groups = [slice(r, r + tile) for r in range(0, x_ref.shape[0], tile)]

    def pre_norm(rows):
        x = x_ref[rows, :]
        xn = x * _rms_scale(x)
        hk_ref[rows, :] = (xn * gk_ref[...]).astype(BF16)
        hq_ref[rows, :] = (xn * gq_ref[...]).astype(BF16)

    def project(h_ref, out_ref, rope, scale, transposed):
        for ti, rows in enumerate(groups):
            for c0 in range(0, tn, 2 * SUB_HEAD_DIM):
                y = jnp.dot(h_ref[rows, :], w_ref[:, c0:c0 + 2 * SUB_HEAD_DIM], preferred_element_type=F32)
                a, b = y[:, :SUB_HEAD_DIM], y[:, SUB_HEAD_DIM:]
                if rope:
                    cos, sin = cos_ref[rows, :], sin_ref[rows, :]
                    a, b = a * cos - b * sin, b * cos + a * sin
                if scale is not None:
                    a, b = a * scale, b * scale
                if transposed:
                    out_ref[ti, c0:c0 + SUB_HEAD_DIM, :] = a.T.astype(BF16)
                    out_ref[ti, c0 + SUB_HEAD_DIM:c0 + 2 * SUB_HEAD_DIM, :] = b.T.astype(BF16)
                else:
                    out_ref[rows, c0:c0 + SUB_HEAD_DIM] = a.astype(BF16)
                    out_ref[rows, c0 + SUB_HEAD_DIM:c0 + 2 * SUB_HEAD_DIM] = b.astype(BF16)

    @pl.when(j == 0)
    def _():
        for rows in groups:
            pre_norm(rows)
        project(hk_ref, k_ref, True, None, False)

    @pl.when((j > 0) & (j < nk))
    def _():
        project(hk_ref, k_ref, True, None, False)

    @pl.when((j >= nk) & (j < nk + nv))
    def _():
        project(hk_ref, v_ref, False, None, True)

    @pl.when(j >= nk + nv)
    def _():
        project(hq_ref, q_ref, True, q_scale, True)


def _qkv_proj(x2d, g_kv, g_q, w_cat, rope_cos, rope_sin, *, d_k, d_v, d_q, q_scale, tile, tm=1024, tn=1024):
    t, d = x2d.shape
    s = rope_cos.shape[0]
    assert t % tm == 0 and s % tm == 0 and d_k % tn == 0 and d_v % tn == 0 and d_q % tn == 0 and tm % tile == 0
    nk, nv, nq = d_k // tn, d_v // tn, d_q // tn
    ns = s // tm
    vec = pl.BlockSpec((1, d), lambda i, j: (0, 0))
    tab = pl.BlockSpec((tm, SUB_HEAD_DIM), lambda i, j: (i % ns, 0))
    k_spec = pl.BlockSpec((tm, tn), lambda i, j: (i, jnp.clip(j, 0, nk - 1)))
    v_spec = pl.BlockSpec((tm // tile, tn, tile), lambda i, j: (i, jnp.clip(j - nk, 0, nv - 1), 0))
    q_spec = pl.BlockSpec((tm // tile, tn, tile), lambda i, j: (i, jnp.clip(j - nk - nv, 0, nq - 1), 0))
    tiled = lambda n: jax.ShapeDtypeStruct((t // tile, n, tile), BF16)
    return pl.pallas_call(
        functools.partial(_qkv_kernel, nk=nk, nv=nv, q_scale=q_scale, tn=tn, tile=tile),
        out_shape=(jax.ShapeDtypeStruct((t, d_k), BF16), tiled(d_v), tiled(d_q)),
        grid=(t // tm, nk + nv + nq),
        in_specs=[
            pl.BlockSpec((tm, d), lambda i, j: (i, 0)),
            vec, vec,
            pl.BlockSpec((d, tn), lambda i, j: (0, j)),
            tab, tab,
        ],
        out_specs=(k_spec, v_spec, q_spec),
        scratch_shapes=[pltpu.VMEM((tm, d), BF16), pltpu.VMEM((tm, d), BF16)],
        compiler_params=pltpu.CompilerParams(
            dimension_semantics=("arbitrary", "arbitrary"),
            vmem_limit_bytes=V7X_VMEM_LIMIT_BYTES),
        name="qkv_proj",
    )(x2d, g_kv.reshape(1, d), g_q.reshape(1, d), w_cat, rope_cos, rope_sin)


def _attn_kernel(q_ref, k_ref, v_ref, member_ref, lq1_ref, lk1_ref, lq2_ref, lk2_ref, g_ref, wo_ref, o_ref,
                 wo_bf16_ref, acc_ref, s_ref, *, tq, nq, lambda_init):
    wo_bf16_ref[...] = wo_ref[...].astype(BF16)
    member = member_ref[...]
    lam = (jnp.exp(jnp.sum(lq1_ref[...] * lk1_ref[...], axis=-1, keepdims=True))
           - jnp.exp(jnp.sum(lq2_ref[...] * lk2_ref[...], axis=-1, keepdims=True))
           + lambda_init)

    def q_tile(qi, diag_slot):
        base = qi % 2
        half = tq // 2
        q = q_ref[0, qi]
        qs = [jnp.where(member == i, q, jnp.zeros_like(q)) for i in range(2)]

        def causal(s):
            kpos = lax.broadcasted_iota(jnp.int32, s.shape, 0)
            qpos = lax.broadcasted_iota(jnp.int32, s.shape, 1)
            return jnp.where(kpos <= qpos, s, -jnp.inf)

        def diag_scores(slot):
            bmax = []
            k = k_ref[0, qi * tq:(qi + 1) * tq, :]
            for i in range(2):
                s_lo = jnp.dot(k[:half], qs[i], preferred_element_type=F32)
                s_lo = jnp.concatenate([causal(s_lo[:, :half]), s_lo[:, half:]], axis=1)
                s_hi = causal(jnp.dot(k[half:], qs[i][:, half:], preferred_element_type=F32))
                s_ref[slot, i, :half, :] = s_lo
                s_ref[slot, i, half:, half:] = s_hi
                b_lo = jnp.max(s_lo, axis=0, keepdims=True)
                b_hi = jnp.max(s_hi, axis=0, keepdims=True)
                bmax.append(jnp.concatenate([b_lo[:, :half], jnp.maximum(b_lo[:, half:], b_hi)], axis=1))
            return tuple(bmax)

        def diag_accumulate(slot, bmax):
            v = v_ref[0, qi]
            new = []
            for i in range(2):
                m_new = bmax[i]
                p_lo = jnp.exp2(s_ref[slot, i, :half, :] - m_new)
                p_hi = jnp.exp2(s_ref[slot, i, half:, half:] - m_new[:, half:])
                l_hi = jnp.sum(p_hi, axis=0, keepdims=True)
                l = jnp.sum(p_lo, axis=0, keepdims=True) + jnp.concatenate([jnp.zeros_like(l_hi), l_hi], axis=1)
                pv = jnp.dot(v[:, :half], p_lo.astype(BF16), preferred_element_type=F32)
                pv_hi = jnp.dot(v[:, half:], p_hi.astype(BF16), preferred_element_type=F32)
                acc_ref[base, i, :, :half] = pv[:, :half]
                acc_ref[base, i, :, half:] = pv[:, half:] + pv_hi
                new += [m_new, l]
            return tuple(new)

        def scores(ki, slot):
            bmax = []
            k = k_ref[0, pl.ds(ki * tq, tq), :]
            for i in range(2):
                s = jnp.dot(k, qs[i], preferred_element_type=F32)
                s_ref[slot, i] = s
                bmax.append(jnp.max(s, axis=0, keepdims=True))
            return tuple(bmax)

        def accumulate(ki, slot, state):
            carry, bmax = state[:4], state[4:]
            v = v_ref[0, ki]
            new = []
            for i in range(2):
                m, l = carry[2 * i], carry[2 * i + 1]
                m_new = jnp.maximum(m, bmax[i])
                alpha = jnp.exp2(m - m_new)
                p = jnp.exp2(s_ref[slot, i] - m_new)
                l = alpha * l + jnp.sum(p, axis=0, keepdims=True)
                pv = jnp.dot(v, p.astype(BF16), preferred_element_type=F32)
                acc_ref[base, i] = alpha * acc_ref[base, i] + pv
                new += [m_new, l]
            return tuple(new)

        bmax = diag_scores(diag_slot)
        slot = diag_slot
        if qi == 0:
            _, l1, _, l2 = diag_accumulate(slot, bmax)
        else:
            slot = 1 - diag_slot
            state = diag_accumulate(diag_slot, bmax) + scores(0, slot)
            first = 0
            if (qi - 1) % 2 == 1:
                state = accumulate(0, slot, state) + scores(1, diag_slot)
                first, slot = 1, diag_slot

            def pair(j, state):
                ki = first + 2 * j
                state = accumulate(ki, slot, state) + scores(ki + 1, 1 - slot)
                return accumulate(ki + 1, 1 - slot, state) + scores(ki + 2, slot)

            if (qi - 1) // 2 > 0:
                state = lax.fori_loop(0, (qi - 1) // 2, pair, state)
            _, l1, _, l2 = accumulate(qi - 1, slot, state)

        ot = acc_ref[base, 0] * (1.0 / l1) - lam * (acc_ref[base, 1] * (1.0 / l2))
        ot = ot * lax.rsqrt(jnp.mean(ot * ot, axis=0, keepdims=True) + EPS)
        o_ref[0, qi * tq:(qi + 1) * tq, :] = ((ot.T * g_ref[...]) * (1.0 - lambda_init)).astype(o_ref.dtype)
        return slot

    last_slot = 1
    for qi in range(nq):
        last_slot = q_tile(qi, 1 - last_slot)


def _diff_attn(q, k, v, lq1, lk1, lq2, lk2, subln_gain, w_o, lambda_init, *, tq=512):
    b, s, d = k.shape
    nh = d // V_HEAD_DIM
    nq = s // tq
    assert q.shape == v.shape == (b, nq, d, tq)
    lvec = pl.BlockSpec((1, SUB_HEAD_DIM), lambda bi, hi: (0, 0))
    hvec = pl.BlockSpec((1, V_HEAD_DIM), lambda bi, hi: (0, 0))
    member = jnp.asarray(_pair_layout()[1], jnp.int32).reshape(V_HEAD_DIM, 1)
    seq = pl.BlockSpec((1, s, V_HEAD_DIM), lambda bi, hi: (bi, 0, hi))
    tiles = pl.BlockSpec((1, nq, V_HEAD_DIM, tq), lambda bi, hi: (bi, 0, hi, 0))
    wo_rows = w_o.shape[0] // (b * nh)
    assert wo_rows * b * nh == w_o.shape[0] and wo_rows % BF16_SUBLANES == 0
    wo_slice = pl.BlockSpec((wo_rows, w_o.shape[1]), lambda bi, hi: (bi * nh + hi, 0))
    return pl.pallas_call(
        functools.partial(_attn_kernel, tq=tq, nq=nq, lambda_init=lambda_init),
        out_shape=(jax.ShapeDtypeStruct((b, s, d), BF16), jax.ShapeDtypeStruct(w_o.shape, BF16)),
        grid=(b, nh),
        in_specs=[tiles, seq, tiles, pl.BlockSpec((V_HEAD_DIM, 1), lambda bi, hi: (0, 0)),
                  lvec, lvec, lvec, lvec, hvec, wo_slice],
        out_specs=(seq, wo_slice),
        scratch_shapes=[pltpu.VMEM((2, 2, V_HEAD_DIM, tq), F32), pltpu.VMEM((2, 2, tq, tq), F32)],
        compiler_params=pltpu.CompilerParams(
            dimension_semantics=("arbitrary", "arbitrary"),
            vmem_limit_bytes=V7X_VMEM_LIMIT_BYTES),
        name="diff_attn",
    )(q, k, v, member, lq1.reshape(1, -1), lk1.reshape(1, -1), lq2.reshape(1, -1), lk2.reshape(1, -1),
      subln_gain.reshape(1, -1), w_o)


def _oproj_kernel(a_ref, x_ref, w_ref, g_ref, o_ref):
    tm = x_ref.shape[0]
    for rows in (slice(0, tm // 2), slice(tm // 2, tm)):
        m = jnp.dot(a_ref[rows, :], w_ref[...], preferred_element_type=F32)
        o_ref[rows, :] = x_ref[rows, :] + (m * _rms_scale(m)) * g_ref[...]


def _out_proj(a2d, x2d, w_o, g_post, *, tm=512):
    t, d = x2d.shape
    assert t % tm == 0
    row = pl.BlockSpec((tm, d), lambda i: (i, 0))
    return pl.pallas_call(
        _oproj_kernel,
        out_shape=jax.ShapeDtypeStruct(x2d.shape, x2d.dtype),
        grid=(t // tm,),
        in_specs=[row, row,
                  pl.BlockSpec((d, d), lambda i: (0, 0)),
                  pl.BlockSpec((1, d), lambda i: (0, 0))],
        out_specs=row,
        compiler_params=pltpu.CompilerParams(
            dimension_semantics=("arbitrary",),
            vmem_limit_bytes=V7X_VMEM_LIMIT_BYTES),
        name="out_proj",
    )(a2d, x2d, w_o, g_post.reshape(1, d))


def _rope_tables(s):
    inv = ROPE_THETA ** (-jnp.arange(0, ROT_DIM, 2, dtype=F32) / ROT_DIM)
    ang = jnp.arange(s, dtype=F32)[:, None] * inv[None, :]
    cos, sin = jnp.cos(ang), jnp.sin(ang)
    rest = SUB_HEAD_DIM - ROT_DIM
    cos = jnp.concatenate([cos, cos, jnp.ones((s, rest), F32)], axis=-1)
    sin = jnp.concatenate([sin, sin, jnp.zeros((s, rest), F32)], axis=-1)
    return cos, sin


def _wprep_kernel(wkv_ref, wq_ref, o_ref, *, nk, nv, tn):
    j = pl.program_id(0)
    half = ROT_DIM // 2

    def pair_layout(w):
        lane = lax.broadcasted_iota(jnp.int32, (1, SUB_HEAD_DIM), 1)
        for c0 in range(0, tn, V_HEAD_DIM):
            a = w[:, c0:c0 + SUB_HEAD_DIM]
            b = w[:, c0 + SUB_HEAD_DIM:c0 + V_HEAD_DIM]
            s0 = jnp.where((lane >= half) & (lane < ROT_DIM), pltpu.roll(b, half, axis=1), a)
            s1 = jnp.where(lane < half, pltpu.roll(a, SUB_HEAD_DIM - half, axis=1), b)
            o_ref[:, c0:c0 + SUB_HEAD_DIM] = s0.astype(BF16)
            o_ref[:, c0 + SUB_HEAD_DIM:c0 + V_HEAD_DIM] = s1.astype(BF16)

    @pl.when(j < nk)
    def _():
        pair_layout(wkv_ref[...])

    @pl.when((j >= nk) & (j < nk + nv))
    def _():
        o_ref[...] = wkv_ref[...].astype(BF16)

    @pl.when(j >= nk + nv)
    def _():
        pair_layout(wq_ref[...])


def _prep_qkv_weights(w_kv, w_q, *, d_k, tn=512):
    d_in, d_kv = w_kv.shape
    d_q = w_q.shape[1]
    assert d_k % tn == 0 and d_kv % tn == 0 and d_q % tn == 0 and tn % V_HEAD_DIM == 0
    nk, nkv, nq = d_k // tn, d_kv // tn, d_q // tn
    return pl.pallas_call(
        functools.partial(_wprep_kernel, nk=nk, nv=nkv - nk, tn=tn),
        out_shape=jax.ShapeDtypeStruct((d_in, d_kv + d_q), BF16),
        grid=(nkv + nq,),
        in_specs=[pl.BlockSpec((d_in, tn), lambda j: (0, jnp.minimum(j, nkv - 1))),
                  pl.BlockSpec((d_in, tn), lambda j: (0, jnp.maximum(j - nkv, 0)))],
        out_specs=pl.BlockSpec((d_in, tn), lambda j: (0, j)),
        compiler_params=pltpu.CompilerParams(
            dimension_semantics=("arbitrary",),
            vmem_limit_bytes=V7X_VMEM_LIMIT_BYTES),
        name="qkv_weight_prep",
    )(w_kv, w_q)


def _lambda_init_for(layer_idx):
    return 0.8 - 0.6 * math.exp(-0.3 * layer_idx)


def kernel(x, norm_mix_pre, norm_mix_post, norm_ffn_pre, norm_ffn_post, w_pool, pool_scale, kv_norm, w_kv, w_q, lambda_q1, lambda_k1, lambda_q2, lambda_k2, subln_gain, w_o, w_ffn_gate, w_ffn_up, w_ffn_down):
    b, s, d = x.shape
    depth = norm_mix_pre.shape[0]
    n_a = w_pool.shape[0]
    rope_cos, rope_sin = _rope_tables(s)
    q_scale = SUB_HEAD_DIM ** -0.5 * LOG2E
    ffn_f32 = (w_ffn_gate, w_ffn_up, w_ffn_down)
    ffn_w = None
    k_sh = v_sh = None
    for l in range(depth):
        if l < n_a:
            x, ffn_w = _pool_mixer(x, norm_mix_pre[l], norm_mix_post[l], w_pool[l].astype(BF16), pool_scale[l],
                                   ffn_f32=(l,) + ffn_f32)
        else:
            jb = l - n_a
            x2d = x.reshape(b * s, d)
            w_cat = _prep_qkv_weights(w_kv, w_q[jb], d_k=d)
            k_l, v_l, q = _qkv_proj(x2d, kv_norm, norm_mix_pre[l], w_cat, rope_cos, rope_sin,
                                    d_k=d, d_v=d, d_q=d, q_scale=q_scale, tile=ATTN_TILE)
            tiled = (b, s // ATTN_TILE, d, ATTN_TILE)
            if k_sh is None:
                k_sh, v_sh = k_l.reshape(b, s, d), v_l.reshape(tiled)
            a, wo_bf16 = _diff_attn(q.reshape(tiled), k_sh, v_sh, lambda_q1[jb], lambda_k1[jb], lambda_q2[jb],
                                    lambda_k2[jb], subln_gain[jb], w_o[jb], _lambda_init_for(l), tq=ATTN_TILE)
            x = _out_proj(a.reshape(b * s, d), x2d, wo_bf16, norm_mix_post[l]).reshape(b, s, d)
        if ffn_w is None:
            ffn_w = tuple(w[l].astype(BF16) for w in ffn_f32)
        nxt = (l + 1,) + ffn_f32 if n_a <= l + 1 < depth else None
        x, ffn_w = _ffn(x.reshape(b * s, d), norm_ffn_pre[l], norm_ffn_post[l], *ffn_w, next_f32=nxt)
        ffn_w = ffn_w or None
        x = x.reshape(b, s, d)
    return x
```
